```python
import math
import jax
import jax.numpy as jnp
from jax import lax
import numpy as np

D_MODEL = 1024
BATCH = 4
SEQ = 8192
DEPTH = 1

ATTN_HEADS = 8
HEAD_DIM = 64
ATTN_WIDTH = ATTN_HEADS * HEAD_DIM
MOBA_BLOCK = 256
MOBA_TOPK = 3
MOBA_QCHUNK = 128
REL_BUCKETS = 32
REL_MAX_DISTANCE = 2048
GMLP_WIDTH = D_MODEL // 2
GMLP_GROUPS = 8
GMLP_GROUP_DIM = GMLP_WIDTH // GMLP_GROUPS
GMLP_CHUNK = 128
PEER_HEADS = 8
PEER_NKEYS = 128
PEER_EXPERTS = PEER_NKEYS * PEER_NKEYS
PEER_QDIM = 128
PEER_HALF = PEER_QDIM // 2
PEER_TOPK = 16
PEER_TOKCHUNK = 128
IN_WIDTH = 3 * ATTN_WIDTH + 2 * GMLP_WIDTH + 2 * D_MODEL
NORM_EPS = 1e-6
NEG_INF = -1e30

kernel_name = "hybrid_moba_gmlp_peer_block"


def rms_norm(x, g):
    xf = x.astype(jnp.float32)
    y = xf * lax.rsqrt(jnp.mean(xf * xf, axis=-1, keepdims=True) + NORM_EPS)
    return (y * g.astype(jnp.float32)).astype(x.dtype)


def layer_norm(x, g, b):
    xf = x.astype(jnp.float32)
    mu = jnp.mean(xf, axis=-1, keepdims=True)
    xc = xf - mu
    var = jnp.mean(xc * xc, axis=-1, keepdims=True)
    y = xc * lax.rsqrt(var + NORM_EPS) * g.astype(jnp.float32) + b.astype(jnp.float32)
    return y.astype(x.dtype)


def t5_bucket(rel):
    max_exact = REL_BUCKETS // 2
    n = jnp.maximum(rel, 0)
    nf = jnp.maximum(n, max_exact).astype(jnp.float32)
    large = max_exact + (jnp.log(nf / max_exact) / math.log(REL_MAX_DISTANCE / max_exact)
                         * (REL_BUCKETS - max_exact)).astype(jnp.int32)
    large = jnp.minimum(large, REL_BUCKETS - 1)
    return jnp.where(n < max_exact, n, large)


def moba_attention(q, k, v, rel_bias):
    B, S, H, dh = q.shape
    nb = -(-S // MOBA_BLOCK)
    pad = nb * MOBA_BLOCK - S
    q = q.transpose(0, 2, 1, 3)
    k = jnp.pad(k.transpose(0, 2, 1, 3), ((0, 0), (0, 0), (0, pad), (0, 0)))
    v = jnp.pad(v.transpose(0, 2, 1, 3), ((0, 0), (0, 0), (0, pad), (0, 0)))
    kb = k.reshape(B, H, nb, MOBA_BLOCK, dh)
    vb = v.reshape(B, H, nb, MOBA_BLOCK, dh)
    scale = dh ** -0.5
    q_blk = jnp.arange(S) // MOBA_BLOCK
    n_slots = min(MOBA_TOPK, nb - 1)
    if n_slots > 0:
        k_mean = jnp.mean(kb.astype(jnp.float32), axis=3)
        score = jnp.einsum('bhsd,bhnd->bhsn', q.astype(jnp.float32), k_mean)
        fully_past = jnp.arange(nb)[None, :] < q_blk[:, None]
        score = jnp.where(fully_past, score, NEG_INF)
        _, sel = lax.top_k(score, n_slots)
    else:
        sel = jnp.zeros((B, H, S, 0), jnp.int32)
    sel_valid = jnp.arange(n_slots)[None, :] < q_blk[:, None]

    nc = S // MOBA_QCHUNK
    q_c = q.reshape(B, H, nc, MOBA_QCHUNK, dh).transpose(2, 0, 1, 3, 4)
    sel_c = sel.reshape(B, H, nc, MOBA_QCHUNK, n_slots).transpose(2, 0, 1, 3, 4)
    valid_c = sel_valid.reshape(nc, MOBA_QCHUNK, n_slots)
    rb = rel_bias.T.astype(jnp.float32)
    b_ix = jnp.arange(B)[:, None, None]
    h_ix = jnp.arange(H)[None, :, None]
    offs = jnp.arange(MOBA_BLOCK)

    def one_chunk(args):
        qc, selc, validc, ci = args
        q_pos = ci * MOBA_QCHUNK + jnp.arange(MOBA_QCHUNK)
        own = (ci * MOBA_QCHUNK) // MOBA_BLOCK
        k_own = lax.dynamic_index_in_dim(kb, own, axis=2, keepdims=False)
        v_own = lax.dynamic_index_in_dim(vb, own, axis=2, keepdims=False)
        rel = q_pos[:, None] - (own * MOBA_BLOCK + offs)[None, :]
        lg = (jnp.einsum('bhqd,bhkd->bhqk', qc, k_own).astype(jnp.float32) * scale
              + rb[:, t5_bucket(rel)])
        logits = [jnp.where(rel >= 0, lg, NEG_INF)]
        for j in range(n_slots):
            idx = selc[..., j]
            k_j = kb[b_ix, h_ix, idx]
            rel_j = q_pos[:, None] - (idx[..., None] * MOBA_BLOCK + offs)
            lg = (jnp.einsum('bhqd,bhqkd->bhqk', qc, k_j).astype(jnp.float32) * scale
                  + rb[h_ix[..., None], t5_bucket(rel_j)])
            logits.append(jnp.where(validc[:, j][:, None], lg, NEG_INF))
        p = jax.nn.softmax(jnp.concatenate(logits, axis=-1), axis=-1).astype(vb.dtype)
        out = jnp.einsum('bhqk,bhkd->bhqd', p[..., :MOBA_BLOCK], v_own)
        for j in range(n_slots):
            v_j = vb[b_ix, h_ix, selc[..., j]]
            out = out + jnp.einsum('bhqk,bhqkd->bhqd',
                                   p[..., (j + 1) * MOBA_BLOCK:(j + 2) * MOBA_BLOCK], v_j)
        return out

    out = lax.map(one_chunk, (q_c, sel_c, valid_c, jnp.arange(nc)))
    return out.transpose(1, 0, 3, 2, 4).reshape(B, S, H * dh)


def spatial_gating(z_u, z_v, ln_g, ln_b, w_s, b_s):
    B, S, _ = z_u.shape
    u = jax.nn.gelu(z_u)
    v = layer_norm(jax.nn.gelu(z_v), ln_g, ln_b)
    nch = S // GMLP_CHUNK
    vc = v.reshape(B, nch, GMLP_CHUNK, GMLP_GROUPS, GMLP_GROUP_DIM)
    causal = jnp.tril(jnp.ones((GMLP_CHUNK, GMLP_CHUNK), dtype=bool))
    w = jnp.where(causal, w_s, jnp.zeros((), w_s.dtype)).astype(v.dtype)
    mixed = jnp.einsum('gts,bnsgc->bntgc', w, vc) + b_s.T[:, :, None].astype(v.dtype)
    return u * mixed.reshape(B, S, GMLP_WIDTH)


def token_mixing(h, w_in, gmlp_ln_g, gmlp_ln_b, gmlp_w_s, gmlp_b_s, rel_bias,
                 w_branch_attn, w_branch_gmlp, w_out):
    B, S, _ = h.shape
    proj = h @ w_in
    a3 = 3 * ATTN_WIDTH
    cuts = [ATTN_WIDTH, 2 * ATTN_WIDTH, a3, a3 + GMLP_WIDTH, a3 + 2 * GMLP_WIDTH,
            a3 + 2 * GMLP_WIDTH + D_MODEL]
    q, k, v, z_u, z_v, g_attn, g_gmlp = jnp.split(proj, cuts, axis=-1)
    heads = (B, S, ATTN_HEADS, HEAD_DIM)
    y_attn = moba_attention(q.reshape(heads), k.reshape(heads), v.reshape(heads), rel_bias)
    y_gmlp = spatial_gating(z_u, z_v, gmlp_ln_g, gmlp_ln_b, gmlp_w_s, gmlp_b_s)
    merged = (jax.nn.sigmoid(g_attn) * (y_attn @ w_branch_attn)
              + jax.nn.sigmoid(g_gmlp) * (y_gmlp @ w_branch_gmlp))
    return merged @ w_out


def peer_ffn(h, w_query, sub_keys, expert_u, expert_v):
    B, S, D = h.shape
    T = B * S
    x = h.reshape(T, D)
    q = (x @ w_query).reshape(T, PEER_HEADS, 2, PEER_HALF).astype(jnp.float32)
    sk = sub_keys.astype(jnp.float32)
    s1 = jnp.einsum('thd,kd->thk', q[:, :, 0], sk[0])
    s2 = jnp.einsum('thd,kd->thk', q[:, :, 1], sk[1])
    v1, i1 = lax.top_k(s1, PEER_TOPK)
    v2, i2 = lax.top_k(s2, PEER_TOPK)
    cand = (v1[..., :, None] + v2[..., None, :]).reshape(T, PEER_HEADS, PEER_TOPK * PEER_TOPK)
    sc, flat = lax.top_k(cand, PEER_TOPK)
    e = (jnp.take_along_axis(i1, flat // PEER_TOPK, axis=-1) * PEER_NKEYS
         + jnp.take_along_axis(i2, flat % PEER_TOPK, axis=-1))
    g = jax.nn.softmax(sc, axis=-1).astype(h.dtype)
    nchunk = T // PEER_TOKCHUNK

    def one_chunk(args):
        xc, ec, gc = args
        u = expert_u[ec]
        a = jax.nn.gelu(jnp.einsum('cd,chkd->chk', xc, u))
        return jnp.einsum('chk,chkd->cd', gc * a, expert_v[ec])

    out = lax.map(one_chunk, (x.reshape(nchunk, PEER_TOKCHUNK, D),
                              e.reshape(nchunk, PEER_TOKCHUNK, PEER_HEADS, PEER_TOPK),
                              g.reshape(nchunk, PEER_TOKCHUNK, PEER_HEADS, PEER_TOPK)))
    return out.reshape(B, S, D)


def setup_inputs(seed: int = 0) -> dict:
    key = jax.random.key(seed)
    ks = jax.random.split(key, 24)
    f32 = jnp.float32
    L = DEPTH

    def nrm(k, shape, scale):
        return jax.random.normal(k, shape, f32) * scale

    return {
        "x": nrm(ks[0], (BATCH, SEQ, D_MODEL), 1.0),
        "c": nrm(ks[1], (BATCH, D_MODEL), 1.0),
        "ada_w": nrm(ks[2], (L, D_MODEL, 6 * D_MODEL), 0.5 * D_MODEL ** -0.5),
        "ada_b": nrm(ks[3], (L, 6 * D_MODEL), 0.02),
        "pre_g1": 1.0 + nrm(ks[4], (L, D_MODEL), 0.02),
        "post_g1": 1.0 + nrm(ks[5], (L, D_MODEL), 0.02),
        "w_in": nrm(ks[6], (L, D_MODEL, IN_WIDTH), D_MODEL ** -0.5),
        "gmlp_ln_g": 1.0 + nrm(ks[7], (L, GMLP_WIDTH), 0.02),
        "gmlp_ln_b": nrm(ks[8], (L, GMLP_WIDTH), 0.02),
        "gmlp_w_s": nrm(ks[9], (L, GMLP_GROUPS, GMLP_CHUNK, GMLP_CHUNK), GMLP_CHUNK ** -0.5),
        "gmlp_b_s": 1.0 + nrm(ks[10], (L, GMLP_GROUPS, GMLP_CHUNK), 0.02),
        "rel_bias": nrm(ks[11], (REL_BUCKETS, ATTN_HEADS), 0.5),
        "w_branch_attn": nrm(ks[12], (L, ATTN_WIDTH, D_MODEL), ATTN_WIDTH ** -0.5),
        "w_branch_gmlp": nrm(ks[13], (L, GMLP_WIDTH, D_MODEL), GMLP_WIDTH ** -0.5),
        "w_out": nrm(ks[14], (L, D_MODEL, D_MODEL), D_MODEL ** -0.5),
        "pre_g2": 1.0 + nrm(ks[15], (L, D_MODEL), 0.02),
        "post_g2": 1.0 + nrm(ks[16], (L, D_MODEL), 0.02),
        "peer_w_query": nrm(ks[17], (L, D_MODEL, PEER_HEADS * PEER_QDIM), D_MODEL ** -0.5),
        "peer_sub_keys": nrm(ks[18], (L, 2, PEER_NKEYS, PEER_HALF), PEER_HALF ** -0.5),
        "peer_u": nrm(ks[19], (L, PEER_EXPERTS, D_MODEL), D_MODEL ** -0.5),
        "peer_v": nrm(ks[20], (L, PEER_EXPERTS, D_MODEL), D_MODEL ** -0.5),
    }


def reference(x, c, ada_w, ada_b, pre_g1, post_g1, w_in, gmlp_ln_g, gmlp_ln_b,
              gmlp_w_s, gmlp_b_s, rel_bias, w_branch_attn, w_branch_gmlp, w_out,
              pre_g2, post_g2, peer_w_query, peer_sub_keys, peer_u, peer_v):
    cond = jax.nn.silu(c)
    for l in range(DEPTH):
        mod = cond @ ada_w[l] + ada_b[l]
        sh1, sc1, gt1, sh2, sc2, gt2 = jnp.split(mod[:, None, :], 6, axis=-1)
        h = rms_norm(x, pre_g1[l]) * (1.0 + sc1) + sh1
        y = token_mixing(h, w_in[l], gmlp_ln_g[l], gmlp_ln_b[l], gmlp_w_s[l], gmlp_b_s[l],
                         rel_bias, w_branch_attn[l], w_branch_gmlp[l], w_out[l])
        x = x + gt1 * rms_norm(y, post_g1[l])
        h = rms_norm(x, pre_g2[l]) * (1.0 + sc2) + sh2
        y = peer_ffn(h, peer_w_query[l], peer_sub_keys[l], peer_u[l], peer_v[l])
        x = x + gt2 * rms_norm(y, post_g2[l])
    return x
```

```python
import functools
import math

import numpy as np
import jax
import jax.numpy as jnp
from jax import lax
from jax.experimental import pallas as pl
from jax.experimental.pallas import tpu as pltpu

F32 = jnp.float32
BF16 = jnp.bfloat16
I32 = jnp.int32

D_MODEL = 1024
ATTN_HEADS = 8
HEAD_DIM = 64
ATTN_WIDTH = ATTN_HEADS * HEAD_DIM
MOBA_BLOCK = 256
MOBA_TOPK = 3
REL_BUCKETS = 32
REL_MAX_DISTANCE = 2048
GMLP_WIDTH = D_MODEL // 2
GMLP_GROUPS = 8
GMLP_GROUP_DIM = GMLP_WIDTH // GMLP_GROUPS
GMLP_CHUNK = 128
PEER_HEADS = 8
PEER_NKEYS = 128
PEER_QDIM = 128
PEER_HALF = PEER_QDIM // 2
PEER_TOPK = 16
PEER_PICKS = PEER_HEADS * PEER_TOPK
IN_WIDTH = 3 * ATTN_WIDTH + 2 * GMLP_WIDTH + 2 * D_MODEL
NORM_EPS = 1e-6
NEG_INF = -1e30

LANES = 128
BIAS_TILES = 10
VMEM_LIMIT = 56 * 1024 * 1024


def _cparams(*sem):
    return pltpu.CompilerParams(dimension_semantics=sem, vmem_limit_bytes=VMEM_LIMIT)


def _rms(xf, g):
    return xf * lax.rsqrt(jnp.mean(xf * xf, axis=-1, keepdims=True) + NORM_EPS) * g


def _ada_kernel(c_ref, w_ref, b_ref, o_ref):
    c = c_ref[...]
    cond = c * jax.nn.sigmoid(c)
    o_ref[...] = jnp.dot(cond, w_ref[...], preferred_element_type=F32,
                         precision=lax.Precision.HIGHEST) + b_ref[...]


def _ada(c8, w, b):
    n = w.shape[1]
    tn = 768
    return pl.pallas_call(
        _ada_kernel,
        grid=(n // tn,),
        in_specs=[pl.BlockSpec((8, D_MODEL), lambda j: (0, 0)),
                  pl.BlockSpec((D_MODEL, tn), lambda j: (0, j)),
                  pl.BlockSpec((1, tn), lambda j: (0, j))],
        out_specs=pl.BlockSpec((8, tn), lambda j: (0, j)),
        out_shape=jax.ShapeDtypeStruct((8, n), F32),
        compiler_params=_cparams("arbitrary"),
    )(c8, w, b)


def _bucket_table():
    max_exact = REL_BUCKETS // 2
    d = np.arange(BIAS_TILES)[:, None, None]
    r = np.arange(MOBA_BLOCK)[None, :, None]
    c = np.arange(MOBA_BLOCK)[None, None, :]
    n = np.maximum(d * MOBA_BLOCK + r - c, 0)
    nf = np.maximum(n, max_exact).astype(np.float32)
    large = max_exact + (np.log(nf / np.float32(max_exact)) / np.float32(math.log(REL_MAX_DISTANCE / max_exact))
                         * np.float32(REL_BUCKETS - max_exact)).astype(np.int32)
    large = np.minimum(large, REL_BUCKETS - 1)
    return np.where(n < max_exact, n, large).astype(np.int32)


def _bias_kernel(rb_ref, bucket_ref, o_ref):
    h = pl.program_id(0)
    for d in range(BIAS_TILES):
        bk = bucket_ref[d]
        acc = jnp.zeros(bk.shape, F32)
        for b in range(REL_BUCKETS):
            acc = jnp.where(bk == b, rb_ref[b, h], acc)
        o_ref[0, d] = acc


def _bias_tiles(rel_bias):
    bucket = jnp.asarray(_bucket_table())
    blk = (BIAS_TILES, MOBA_BLOCK, MOBA_BLOCK)
    return pl.pallas_call(
        _bias_kernel,
        grid=(ATTN_HEADS,),
        in_specs=[pl.BlockSpec(memory_space=pltpu.SMEM),
                  pl.BlockSpec(blk, lambda h: (0, 0, 0))],
        out_specs=pl.BlockSpec((1,) + blk, lambda h: (h, 0, 0, 0)),
        out_shape=jax.ShapeDtypeStruct((ATTN_HEADS,) + blk, F32),
        compiler_params=_cparams("arbitrary"),
    )(rel_bias, bucket)


IN_TM = 256
IN_NCHUNK = 512


def _inproj_kernel(x_ref, sc_ref, sh_ref, g_ref, w_ref,
                   q_ref, k_ref, v_ref, zu_ref, zv_ref, ga_ref, gg_ref):
    x = x_ref[...]
    h = _rms(x, g_ref[...]) * (1.0 + sc_ref[0]) + sh_ref[0]
    hb = h.astype(BF16)
    outs = ((q_ref, ATTN_WIDTH), (k_ref, ATTN_WIDTH), (v_ref, ATTN_WIDTH),
            (zu_ref, GMLP_WIDTH), (zv_ref, GMLP_WIDTH), (ga_ref, D_MODEL), (gg_ref, D_MODEL))
    off = 0
    for ref, width in outs:
        for c0 in range(0, width, IN_NCHUNK):
            p = jnp.dot(hb, w_ref[:, off + c0:off + c0 + IN_NCHUNK], preferred_element_type=F32)
            ref[:, c0:c0 + IN_NCHUNK] = p.astype(ref.dtype)
        off += width


def _inproj(x2, sc1, sh1, pre_g, w_in_bf, seq):
    t = x2.shape[0]
    tm = IN_TM
    per_b = seq // tm
    row = lambda w: pl.BlockSpec((tm, w), lambda i: (i, 0))
    mod = pl.BlockSpec((1, 1, D_MODEL), lambda i: (i // per_b, 0, 0))
    return pl.pallas_call(
        _inproj_kernel,
        grid=(t // tm,),
        in_specs=[row(D_MODEL), mod, mod,
                  pl.BlockSpec((1, D_MODEL), lambda i: (0, 0)),
                  pl.BlockSpec((D_MODEL, IN_WIDTH), lambda i: (0, 0))],
        out_specs=[row(ATTN_WIDTH), row(ATTN_WIDTH), row(ATTN_WIDTH),
                   row(GMLP_WIDTH), row(GMLP_WIDTH), row(D_MODEL), row(D_MODEL)],
        out_shape=[jax.ShapeDtypeStruct((t, ATTN_WIDTH), BF16)] * 3
        + [jax.ShapeDtypeStruct((t, GMLP_WIDTH), F32)] * 2
        + [jax.ShapeDtypeStruct((t, D_MODEL), F32)] * 2,
        compiler_params=_cparams("arbitrary"),
    )(x2, sc1, sh1, pre_g, w_in_bf)


GMLP_TM = 512


def _gmlp_kernel(zu_ref, zv_ref, lg_ref, lb_ref, ws_ref, bs_ref, y_ref):
    u = jax.nn.gelu(zu_ref[...])
    gv = jax.nn.gelu(zv_ref[...])
    mu = jnp.mean(gv, axis=-1, keepdims=True)
    vc = gv - mu
    var = jnp.mean(vc * vc, axis=-1, keepdims=True)
    v = (vc * lax.rsqrt(var + NORM_EPS) * lg_ref[...] + lb_ref[...]).astype(BF16)
    tri = (lax.broadcasted_iota(I32, (GMLP_CHUNK, GMLP_CHUNK), 0)
           >= lax.broadcasted_iota(I32, (GMLP_CHUNK, GMLP_CHUNK), 1))
    lane = lax.broadcasted_iota(I32, (GMLP_CHUNK, LANES), 1)
    lo = lane < GMLP_GROUP_DIM
    zero = jnp.zeros((), BF16)
    for pair in range(GMLP_GROUPS // 2):
        w0 = jnp.where(tri, ws_ref[2 * pair], 0.0).astype(BF16)
        w1 = jnp.where(tri, ws_ref[2 * pair + 1], 0.0).astype(BF16)
        b0 = bs_ref[:, 2 * pair:2 * pair + 1]
        b1 = bs_ref[:, 2 * pair + 1:2 * pair + 2]
        bias = jnp.where(lo, b0, b1)
        cols = slice(pair * LANES, (pair + 1) * LANES)
        for ch in range(GMLP_TM // GMLP_CHUNK):
            rows = slice(ch * GMLP_CHUNK, (ch + 1) * GMLP_CHUNK)
            v2 = v[rows, cols]
            mixed = (jnp.dot(w0, jnp.where(lo, v2, zero), preferred_element_type=F32)
                     + jnp.dot(w1, jnp.where(lo, zero, v2), preferred_element_type=F32) + bias)
            y_ref[rows, cols] = (u[rows, cols] * mixed).astype(y_ref.dtype)


def _gmlp(zu, zv, ln_g, ln_b, w_s, bs_t):
    t = zu.shape[0]
    tm = GMLP_TM
    row = pl.BlockSpec((tm, GMLP_WIDTH), lambda i: (i, 0))
    vec = pl.BlockSpec((1, GMLP_WIDTH), lambda i: (0, 0))
    return pl.pallas_call(
        _gmlp_kernel,
        grid=(t // tm,),
        in_specs=[row, row, vec, vec,
                  pl.BlockSpec((GMLP_GROUPS, GMLP_CHUNK, GMLP_CHUNK), lambda i: (0, 0, 0)),
                  pl.BlockSpec((GMLP_CHUNK, GMLP_GROUPS), lambda i: (0, 0))],
        out_specs=row,
        out_shape=jax.ShapeDtypeStruct((t, GMLP_WIDTH), BF16),
        compiler_params=_cparams("arbitrary"),
    )(zu, zv, ln_g, ln_b, w_s, bs_t)


def _moba_kernel(q_ref, k_ref, v_ref, bias_ref, o_ref, kmean_sc, *, nb, n_slots):
    i = pl.program_id(2)
    blk = MOBA_BLOCK
    scale = HEAD_DIM ** -0.5
    nt = (((1,), (1,)), ((), ()))

    @pl.when(i == 0)
    def _():
        def body(j, c):
            kb = k_ref[pl.ds(pl.multiple_of(j * blk, blk), blk), :].astype(F32)
            kmean_sc[pl.ds(j, 1), :] = jnp.mean(kb, axis=0, keepdims=True)
            return c
        lax.fori_loop(0, nb, body, 0)

    q2 = q_ref[...]
    km = kmean_sc[...].astype(BF16)
    lane_q = lax.broadcasted_iota(I32, (blk, LANES), 1)
    blk_id = lax.broadcasted_iota(I32, (blk, nb), 1)
    row_i = lax.broadcasted_iota(I32, (blk, blk), 0)
    col_i = lax.broadcasted_iota(I32, (blk, blk), 1)
    own0 = pl.multiple_of(i * blk, blk)
    k_own = k_ref[pl.ds(own0, blk), :]
    v_own = v_ref[pl.ds(own0, blk), :]
    zero = jnp.zeros((), BF16)

    outs = []
    for hh in range(2):
        head_lanes = (lane_q < HEAD_DIM) if hh == 0 else (lane_q >= HEAD_DIM)
        qh = jnp.where(head_lanes, q2, zero)
        score = lax.dot_general(qh, km, nt, preferred_element_type=F32)
        score = jnp.where(blk_id < i, score, NEG_INF)
        sels = []
        for r in range(n_slots):
            m = jnp.max(score, axis=-1, keepdims=True)
            idx = jnp.min(jnp.where(score == m, blk_id, nb), axis=-1, keepdims=True)
            sels.append(jnp.where(i > r, idx, -1))
            score = jnp.where(blk_id == idx, -jnp.inf, score)
        s = lax.dot_general(qh, k_own, nt, preferred_element_type=F32) * scale + bias_ref[hh, 0]
        s = jnp.where(row_i >= col_i, s, NEG_INF)
        m0 = jnp.max(s, axis=-1, keepdims=True)
        p = jnp.exp(s - m0)
        l0 = jnp.sum(p, axis=-1, keepdims=True)
        acc0 = jnp.dot(p.astype(BF16), v_own, preferred_element_type=F32)

        def body(j, carry, qh=qh, sels=sels, hh=hh):
            m_run, l_run, acc = carry
            j0 = pl.multiple_of(j * blk, blk)
            kb = k_ref[pl.ds(j0, blk), :]
            vb = v_ref[pl.ds(j0, blk), :]
            dist = jnp.minimum(i - j, BIAS_TILES - 1)
            hit = sels[0] == j
            for sr in sels[1:]:
                hit = hit | (sr == j)
            pen = jnp.where(hit, 0.0, NEG_INF)
            s = (lax.dot_general(qh, kb, nt, preferred_element_type=F32) * scale
                 + bias_ref[hh, dist] + pen)
            m_new = jnp.maximum(m_run, jnp.max(s, axis=-1, keepdims=True))
            alpha = jnp.exp(m_run - m_new)
            p = jnp.exp(s - m_new)
            l_new = alpha * l_run + jnp.sum(p, axis=-1, keepdims=True)
            acc_new = alpha * acc + jnp.dot(p.astype(BF16), vb, preferred_element_type=F32)
            return m_new, l_new, acc_new

        if n_slots > 0:
            _, l_fin, acc_fin = lax.fori_loop(0, i, body, (m0, l0, acc0))
        else:
            l_fin, acc_fin = l0, acc0
        outs.append(acc_fin / l_fin)

    o_ref[...] = jnp.where(lane_q < HEAD_DIM, outs[0], outs[1]).astype(o_ref.dtype)


def _moba(q, k, v, bias, batch, seq):
    t = q.shape[0]
    nb = seq // MOBA_BLOCK
    n_slots = min(MOBA_TOPK, nb - 1)
    pairs = ATTN_HEADS // 2
    kv = pl.BlockSpec((seq, LANES), lambda hp, b, i: (b, hp))
    qo = pl.BlockSpec((MOBA_BLOCK, LANES), lambda hp, b, i: (b * nb + i, hp))
    return pl.pallas_call(
        functools.partial(_moba_kernel, nb=nb, n_slots=n_slots),
        grid=(pairs, batch, nb),
        in_specs=[qo, kv, kv,
                  pl.BlockSpec((2, BIAS_TILES, MOBA_BLOCK, MOBA_BLOCK), lambda hp, b, i: (hp, 0, 0, 0))],
        out_specs=qo,
        out_shape=jax.ShapeDtypeStruct((t, ATTN_WIDTH), BF16),
        scratch_shapes=[pltpu.VMEM((nb, LANES), F32)],
        compiler_params=_cparams("arbitrary", "arbitrary", "arbitrary"),
    )(q, k, v, bias)


MERGE_TM = 256


def _merge_kernel(x_ref, ya_ref, yg_ref, ga_ref, gg_ref, gt1_ref, sc2_ref, sh2_ref,
                  pg1_ref, g2_ref, wa_ref, wg_ref, wo_ref, wq_ref, sk0_ref, sk1_ref,
                  x1_ref, h2_ref, st_ref):
    nt = (((1,), (1,)), ((), ()))
    la = jnp.dot(ya_ref[...], wa_ref[...], preferred_element_type=F32)
    lg = jnp.dot(yg_ref[...], wg_ref[...], preferred_element_type=F32)
    merged = jax.nn.sigmoid(ga_ref[...]) * la + jax.nn.sigmoid(gg_ref[...]) * lg
    y = jnp.dot(merged.astype(BF16), wo_ref[...], preferred_element_type=F32)
    x1 = x_ref[...] + gt1_ref[0] * _rms(y, pg1_ref[...])
    x1_ref[...] = x1
    h2 = _rms(x1, g2_ref[...]) * (1.0 + sc2_ref[0]) + sh2_ref[0]
    h2_ref[...] = h2
    qp = jnp.dot(h2.astype(BF16), wq_ref[...], preferred_element_type=F32).astype(BF16)
    for h in range(PEER_HEADS):
        qh = qp[:, h * PEER_QDIM:(h + 1) * PEER_QDIM]
        st_ref[2 * h] = lax.dot_general(sk0_ref[...], qh, nt, preferred_element_type=F32)
        st_ref[2 * h + 1] = lax.dot_general(sk1_ref[...], qh, nt, preferred_element_type=F32)


def _merge(x2, ya, yg, ga, gg, gt1, sc2, sh2, post_g1, pre_g2, wa, wg, wo, wq, sk0, sk1, seq):
    t = x2.shape[0]
    tm = MERGE_TM
    per_b = seq // tm
    row = lambda w: pl.BlockSpec((tm, w), lambda i: (i, 0))
    mod = pl.BlockSpec((1, 1, D_MODEL), lambda i: (i // per_b, 0, 0))
    vec = pl.BlockSpec((1, D_MODEL), lambda i: (0, 0))
    full = lambda a: pl.BlockSpec(a.shape, lambda i: (0,) * a.ndim)
    return pl.pallas_call(
        _merge_kernel,
        grid=(t // tm,),
        in_specs=[row(D_MODEL), row(ATTN_WIDTH), row(GMLP_WIDTH), row(D_MODEL), row(D_MODEL),
                  mod, mod, mod, vec, vec, full(wa), full(wg), full(wo), full(wq), full(sk0), full(sk1)],
        out_specs=[row(D_MODEL), row(D_MODEL),
                   pl.BlockSpec((2 * PEER_HEADS, PEER_NKEYS, tm), lambda i: (0, 0, i))],
        out_shape=[jax.ShapeDtypeStruct((t, D_MODEL), F32), jax.ShapeDtypeStruct((t, D_MODEL), F32),
                   jax.ShapeDtypeStruct((2 * PEER_HEADS, PEER_NKEYS, t), F32)],
        compiler_params=_cparams("arbitrary"),
    )(x2, ya, yg, ga, gg, gt1, sc2, sh2, post_g1, pre_g2, wa, wg, wo, wq, sk0, sk1)


ROUTE_SUB = 8
_CANDS = [(a, b) for a in range(PEER_TOPK) for b in range(PEER_TOPK) if (a + 1) * (b + 1) <= PEER_TOPK]


def _route_kernel(s_ref, e_ref, g_ref, buf):
    kio = lax.broadcasted_iota(I32, (PEER_NKEYS, ROUTE_SUB, LANES), 0)

    def top16(half):
        buf[...] = s_ref[half]
        vals, idxs = [], []
        for _ in range(PEER_TOPK):
            s = buf[...]
            m = jnp.max(s, axis=0)
            idx = jnp.min(jnp.where(s == m[None], kio, PEER_NKEYS), axis=0)
            buf[...] = jnp.where(kio == idx[None], -jnp.inf, s)
            vals.append(m)
            idxs.append(idx)
        return vals, idxs

    v1, i1 = top16(0)
    v2, i2 = top16(1)
    cand = [v1[a] + v2[b] for a, b in _CANDS]
    eid = [i1[a] * PEER_NKEYS + i2[b] for a, b in _CANDS]
    flat = [a * PEER_TOPK + b for a, b in _CANDS]
    sc, es = [], []
    for _ in range(PEER_TOPK):
        m = functools.reduce(jnp.maximum, cand)
        idx = functools.reduce(jnp.minimum,
                               [jnp.where(c == m, f, PEER_TOPK * PEER_TOPK) for c, f in zip(cand, flat)])
        es.append(functools.reduce(jnp.maximum, [jnp.where(idx == f, e, -1) for e, f in zip(eid, flat)]))
        cand = [jnp.where(idx == f, -jnp.inf, c) for c, f in zip(cand, flat)]
        sc.append(m)
    ex = [jnp.exp(s - sc[0]) for s in sc]
    tot = functools.reduce(jnp.add, ex)
    for r in range(PEER_TOPK):
        e_ref[0, r] = es[r]
        g_ref[0, r] = ex[r] / tot


def _route(st4):
    ntile = st4.shape[2]
    grp = ROUTE_SUB
    out = pl.BlockSpec((1, PEER_TOPK, grp, LANES), lambda h, g: (h, 0, g, 0))
    shp = (PEER_HEADS, PEER_TOPK, ntile, LANES)
    return pl.pallas_call(
        _route_kernel,
        grid=(PEER_HEADS, ntile // grp),
        in_specs=[pl.BlockSpec((2, PEER_NKEYS, grp, LANES), lambda h, g: (h, 0, g, 0))],
        out_specs=[out, out],
        out_shape=[jax.ShapeDtypeStruct(shp, I32), jax.ShapeDtypeStruct(shp, F32)],
        scratch_shapes=[pltpu.VMEM((PEER_NKEYS, grp, LANES), F32)],
        compiler_params=_cparams("arbitrary", "arbitrary"),
    )(st4)


PEER_TT = 128
PEER_NBUF = 4


def _peer_kernel(e_ref, h2_ref, gt_ref, x1_ref, gt2_ref, pg2_ref, uv_ref, o_ref, buf, sem, y_sc):
    tt = PEER_TT

    def issue(t, slot):
        for k in range(PEER_PICKS):
            pltpu.make_async_copy(uv_ref.at[e_ref[k, t]], buf.at[slot, k], sem.at[slot]).start()

    def wait(slot):
        pltpu.make_async_copy(uv_ref.at[pl.ds(0, PEER_PICKS)], buf.at[slot], sem.at[slot]).wait()

    for t in range(PEER_NBUF - 1):
        issue(t, t)

    lane_t = lax.broadcasted_iota(I32, (PEER_PICKS, tt), 1)

    def body(t, c):
        slot = lax.rem(t, PEER_NBUF)
        nxt = t + PEER_NBUF - 1

        @pl.when(nxt < tt)
        def _():
            issue(nxt, lax.rem(nxt, PEER_NBUF))

        wait(slot)
        xb = h2_ref[pl.ds(t, 1), :]
        u = buf[slot, :, 0:D_MODEL]
        a = jnp.sum(u * xb, axis=-1, keepdims=True)
        gcol = jnp.sum(jnp.where(lane_t == t, gt_ref[...], 0.0), axis=-1, keepdims=True)
        w = gcol * jax.nn.gelu(a)
        v = buf[slot, :, D_MODEL:2 * D_MODEL]
        y_sc[pl.ds(t, 1), :] = jnp.sum(w * v, axis=0, keepdims=True)
        return c

    lax.fori_loop(0, tt, body, 0)
    o_ref[...] = x1_ref[...] + gt2_ref[0] * _rms(y_sc[...], pg2_ref[...])


def _peer(e_t, h2, g_t, x1, gt2, post_g2, uv, seq):
    t = h2.shape[0]
    tt = PEER_TT
    per_b = seq // tt
    row = pl.BlockSpec((tt, D_MODEL), lambda i: (i, 0))
    return pl.pallas_call(
        _peer_kernel,
        grid=(t // tt,),
        in_specs=[pl.BlockSpec((PEER_PICKS, tt), lambda i: (0, i), memory_space=pltpu.SMEM),
                  row,
                  pl.BlockSpec((PEER_PICKS, tt), lambda i: (0, i)),
                  row,
                  pl.BlockSpec((1, 1, D_MODEL), lambda i: (i // per_b, 0, 0)),
                  pl.BlockSpec((1, D_MODEL), lambda i: (0, 0)),
                  pl.BlockSpec(memory_space=pl.ANY)],
        out_specs=row,
        out_shape=jax.ShapeDtypeStruct((t, D_MODEL), F32),
        scratch_shapes=[pltpu.VMEM((PEER_NBUF, PEER_PICKS, 2 * D_MODEL), F32),
                        pltpu.SemaphoreType.DMA((PEER_NBUF,)),
                        pltpu.VMEM((tt, D_MODEL), F32)],
        compiler_params=_cparams("arbitrary"),
    )(e_t, h2, g_t, x1, gt2, post_g2, uv)


def kernel(x, c, ada_w, ada_b, pre_g1, post_g1, w_in, gmlp_ln_g, gmlp_ln_b, gmlp_w_s, gmlp_b_s,
           rel_bias, w_branch_attn, w_branch_gmlp, w_out, pre_g2, post_g2, peer_w_query,
           peer_sub_keys, peer_u, peer_v):
    batch, seq, d = x.shape
    depth = ada_w.shape[0]
    t = batch * seq
    assert d == D_MODEL and batch <= 8 and seq % (2 * MOBA_BLOCK) == 0 and t % (ROUTE_SUB * LANES) == 0

    c8 = jnp.zeros((8, d), F32).at[:batch].set(c)
    bias = _bias_tiles(rel_bias)
    x2 = x.reshape(t, d)
    for l in range(depth):
        mod = _ada(c8, ada_w[l], ada_b[l][None, :])[:batch]
        sh1, sc1, gt1, sh2, sc2, gt2 = [m[:, None, :] for m in jnp.split(mod, 6, axis=-1)]

        q, k, v, zu, zv, ga, gg = _inproj(x2, sc1, sh1, pre_g1[l][None, :], w_in[l].astype(BF16), seq)
        yg = _gmlp(zu, zv, gmlp_ln_g[l][None, :], gmlp_ln_b[l][None, :], gmlp_w_s[l], gmlp_b_s[l].T)
        ya = _moba(q, k, v, bias, batch, seq)

        zpad = jnp.zeros((PEER_NKEYS, PEER_HALF), F32)
        sk0 = jnp.concatenate([peer_sub_keys[l, 0], zpad], axis=1).astype(BF16)
        sk1 = jnp.concatenate([zpad, peer_sub_keys[l, 1]], axis=1).astype(BF16)
        x1, h2, st = _merge(x2, ya, yg, ga, gg, gt1, sc2, sh2, post_g1[l][None, :], pre_g2[l][None, :],
                            w_branch_attn[l].astype(BF16), w_branch_gmlp[l].astype(BF16),
                            w_out[l].astype(BF16), peer_w_query[l].astype(BF16), sk0, sk1, seq)

        e4, g4 = _route(st.reshape(2 * PEER_HEADS, PEER_NKEYS, t // LANES, LANES))
        e_t = e4.reshape(PEER_PICKS, t)
        g_t = g4.reshape(PEER_PICKS, t)
        uv = jnp.concatenate([peer_u[l], peer_v[l]], axis=1)
        x2 = _peer(e_t, h2, g_t, x1, gt2, post_g2[l][None, :], uv, seq)
    return x2.reshape(batch, seq, d)
```

```python
import functools
import math

import numpy as np
import jax
import jax.numpy as jnp
from jax import lax
from jax.experimental import pallas as pl
from jax.experimental.pallas import tpu as pltpu

F32 = jnp.float32
BF16 = jnp.bfloat16
I32 = jnp.int32

D_MODEL = 1024
ATTN_HEADS = 8
HEAD_DIM = 64
ATTN_WIDTH = ATTN_HEADS * HEAD_DIM
MOBA_BLOCK = 256
MOBA_TOPK = 3
REL_BUCKETS = 32
REL_MAX_DISTANCE = 2048
GMLP_WIDTH = D_MODEL // 2
GMLP_GROUPS = 8
GMLP_GROUP_DIM = GMLP_WIDTH // GMLP_GROUPS
GMLP_CHUNK = 128
PEER_HEADS = 8
PEER_NKEYS = 128
PEER_QDIM = 128
PEER_HALF = PEER_QDIM // 2
PEER_TOPK = 16
PEER_PICKS = PEER_HEADS * PEER_TOPK
IN_WIDTH = 3 * ATTN_WIDTH + 2 * GMLP_WIDTH + 2 * D_MODEL
NORM_EPS = 1e-6
NEG_INF = -1e30

LANES = 128
BIAS_TILES = 10
VMEM_LIMIT = 56 * 1024 * 1024


def _cparams(*sem):
    return pltpu.CompilerParams(dimension_semantics=sem, vmem_limit_bytes=VMEM_LIMIT)


def _rms(xf, g):
    return xf * lax.rsqrt(jnp.mean(xf * xf, axis=-1, keepdims=True) + NORM_EPS) * g


def _ada_kernel(c_ref, w_ref, b_ref, o_ref):
    c = c_ref[...]
    cond = c * jax.nn.sigmoid(c)
    o_ref[...] = jnp.dot(cond, w_ref[...], preferred_element_type=F32,
                         precision=lax.Precision.HIGHEST) + b_ref[...]


def _ada(c8, w, b):
    n = w.shape[1]
    tn = 768
    return pl.pallas_call(
        _ada_kernel,
        grid=(n // tn,),
        in_specs=[pl.BlockSpec((8, D_MODEL), lambda j: (0, 0)),
                  pl.BlockSpec((D_MODEL, tn), lambda j: (0, j)),
                  pl.BlockSpec((1, tn), lambda j: (0, j))],
        out_specs=pl.BlockSpec((8, tn), lambda j: (0, j)),
        out_shape=jax.ShapeDtypeStruct((8, n), F32),
        compiler_params=_cparams("arbitrary"),
    )(c8, w, b)


def _bucket_table():
    max_exact = REL_BUCKETS // 2
    d = np.arange(BIAS_TILES)[:, None, None]
    r = np.arange(MOBA_BLOCK)[None, :, None]
    c = np.arange(MOBA_BLOCK)[None, None, :]
    n = np.maximum(d * MOBA_BLOCK + r - c, 0)
    nf = np.maximum(n, max_exact).astype(np.float32)
    large = max_exact + (np.log(nf / np.float32(max_exact)) / np.float32(math.log(REL_MAX_DISTANCE / max_exact))
                         * np.float32(REL_BUCKETS - max_exact)).astype(np.int32)
    large = np.minimum(large, REL_BUCKETS - 1)
    return np.where(n < max_exact, n, large).astype(np.int32)


def _bias_kernel(rb_ref, bucket_ref, o_ref):
    h = pl.program_id(0)
    for d in range(BIAS_TILES):
        bk = bucket_ref[d]
        acc = jnp.zeros(bk.shape, F32)
        for b in range(REL_BUCKETS):
            acc = jnp.where(bk == b, rb_ref[b, h], acc)
        o_ref[0, d] = acc


def _bias_tiles(rel_bias):
    bucket = jnp.asarray(_bucket_table())
    blk = (BIAS_TILES, MOBA_BLOCK, MOBA_BLOCK)
    return pl.pallas_call(
        _bias_kernel,
        grid=(ATTN_HEADS,),
        in_specs=[pl.BlockSpec(memory_space=pltpu.SMEM),
                  pl.BlockSpec(blk, lambda h: (0, 0, 0))],
        out_specs=pl.BlockSpec((1,) + blk, lambda h: (h, 0, 0, 0)),
        out_shape=jax.ShapeDtypeStruct((ATTN_HEADS,) + blk, F32),
        compiler_params=_cparams("arbitrary"),
    )(rel_bias, bucket)


IN_TM = 256
IN_NCHUNK = 512


def _inproj_kernel(x_ref, sc_ref, sh_ref, g_ref, w_ref,
                   q_ref, k_ref, v_ref, zu_ref, zv_ref, ga_ref, gg_ref):
    x = x_ref[...]
    h = _rms(x, g_ref[...]) * (1.0 + sc_ref[0]) + sh_ref[0]
    hb = h.astype(BF16)
    outs = ((q_ref, ATTN_WIDTH), (k_ref, ATTN_WIDTH), (v_ref, ATTN_WIDTH),
            (zu_ref, GMLP_WIDTH), (zv_ref, GMLP_WIDTH), (ga_ref, D_MODEL), (gg_ref, D_MODEL))
    off = 0
    for ref, width in outs:
        for c0 in range(0, width, IN_NCHUNK):
            p = jnp.dot(hb, w_ref[:, off + c0:off + c0 + IN_NCHUNK], preferred_element_type=F32)
            ref[:, c0:c0 + IN_NCHUNK] = p.astype(ref.dtype)
        off += width


def _inproj(x2, sc1, sh1, pre_g, w_in_bf, seq):
    t = x2.shape[0]
    tm = IN_TM
    per_b = seq // tm
    row = lambda w: pl.BlockSpec((tm, w), lambda i: (i, 0))
    mod = pl.BlockSpec((1, 1, D_MODEL), lambda i: (i // per_b, 0, 0))
    return pl.pallas_call(
        _inproj_kernel,
        grid=(t // tm,),
        in_specs=[row(D_MODEL), mod, mod,
                  pl.BlockSpec((1, D_MODEL), lambda i: (0, 0)),
                  pl.BlockSpec((D_MODEL, IN_WIDTH), lambda i: (0, 0))],
        out_specs=[row(ATTN_WIDTH), row(ATTN_WIDTH), row(ATTN_WIDTH),
                   row(GMLP_WIDTH), row(GMLP_WIDTH), row(D_MODEL), row(D_MODEL)],
        out_shape=[jax.ShapeDtypeStruct((t, ATTN_WIDTH), BF16)] * 3
        + [jax.ShapeDtypeStruct((t, GMLP_WIDTH), F32)] * 2
        + [jax.ShapeDtypeStruct((t, D_MODEL), F32)] * 2,
        compiler_params=_cparams("arbitrary"),
    )(x2, sc1, sh1, pre_g, w_in_bf)


GMLP_TM = 512


def _gmlp_kernel(zu_ref, zv_ref, lg_ref, lb_ref, ws_ref, bs_ref, y_ref):
    u = jax.nn.gelu(zu_ref[...])
    gv = jax.nn.gelu(zv_ref[...])
    mu = jnp.mean(gv, axis=-1, keepdims=True)
    vc = gv - mu
    var = jnp.mean(vc * vc, axis=-1, keepdims=True)
    v = (vc * lax.rsqrt(var + NORM_EPS) * lg_ref[...] + lb_ref[...]).astype(BF16)
    tri = (lax.broadcasted_iota(I32, (GMLP_CHUNK, GMLP_CHUNK), 0)
           >= lax.broadcasted_iota(I32, (GMLP_CHUNK, GMLP_CHUNK), 1))
    lane = lax.broadcasted_iota(I32, (GMLP_CHUNK, LANES), 1)
    lo = lane < GMLP_GROUP_DIM
    zero = jnp.zeros((), BF16)
    for pair in range(GMLP_GROUPS // 2):
        w0 = jnp.where(tri, ws_ref[2 * pair], 0.0).astype(BF16)
        w1 = jnp.where(tri, ws_ref[2 * pair + 1], 0.0).astype(BF16)
        b0 = bs_ref[:, 2 * pair:2 * pair + 1]
        b1 = bs_ref[:, 2 * pair + 1:2 * pair + 2]
        bias = jnp.where(lo, b0, b1)
        cols = slice(pair * LANES, (pair + 1) * LANES)
        for ch in range(GMLP_TM // GMLP_CHUNK):
            rows = slice(ch * GMLP_CHUNK, (ch + 1) * GMLP_CHUNK)
            v2 = v[rows, cols]
            mixed = (jnp.dot(w0, jnp.where(lo, v2, zero), preferred_element_type=F32)
                     + jnp.dot(w1, jnp.where(lo, zero, v2), preferred_element_type=F32) + bias)
            y_ref[rows, cols] = (u[rows, cols] * mixed).astype(y_ref.dtype)


def _gmlp(zu, zv, ln_g, ln_b, w_s, bs_t):
    t = zu.shape[0]
    tm = GMLP_TM
    row = pl.BlockSpec((tm, GMLP_WIDTH), lambda i: (i, 0))
    vec = pl.BlockSpec((1, GMLP_WIDTH), lambda i: (0, 0))
    return pl.pallas_call(
        _gmlp_kernel,
        grid=(t // tm,),
        in_specs=[row, row, vec, vec,
                  pl.BlockSpec((GMLP_GROUPS, GMLP_CHUNK, GMLP_CHUNK), lambda i: (0, 0, 0)),
                  pl.BlockSpec((GMLP_CHUNK, GMLP_GROUPS), lambda i: (0, 0))],
        out_specs=row,
        out_shape=jax.ShapeDtypeStruct((t, GMLP_WIDTH), BF16),
        compiler_params=_cparams("arbitrary"),
    )(zu, zv, ln_g, ln_b, w_s, bs_t)


def _moba_kernel(q_ref, k_ref, v_ref, bias_ref, o_ref, kmean_sc, *, nb, n_slots):
    i = pl.program_id(2)
    blk = MOBA_BLOCK
    scale = HEAD_DIM ** -0.5
    nt = (((1,), (1,)), ((), ()))

    @pl.when(i == 0)
    def _():
        def body(j, c):
            kb = k_ref[pl.ds(pl.multiple_of(j * blk, blk), blk), :].astype(F32)
            kmean_sc[pl.ds(j, 1), :] = jnp.mean(kb, axis=0, keepdims=True)
            return c
        lax.fori_loop(0, nb, body, 0)

    q2 = q_ref[...]
    km = kmean_sc[...].astype(BF16)
    lane_q = lax.broadcasted_iota(I32, (blk, LANES), 1)
    blk_id = lax.broadcasted_iota(I32, (blk, nb), 1)
    row_i = lax.broadcasted_iota(I32, (blk, blk), 0)
    col_i = lax.broadcasted_iota(I32, (blk, blk), 1)
    own0 = pl.multiple_of(i * blk, blk)
    k_own = k_ref[pl.ds(own0, blk), :]
    v_own = v_ref[pl.ds(own0, blk), :]
    zero = jnp.zeros((), BF16)

    outs = []
    for hh in range(2):
        head_lanes = (lane_q < HEAD_DIM) if hh == 0 else (lane_q >= HEAD_DIM)
        qh = jnp.where(head_lanes, q2, zero)
        score = lax.dot_general(qh, km, nt, preferred_element_type=F32)
        score = jnp.where(blk_id < i, score, NEG_INF)
        sels = []
        for r in range(n_slots):
            m = jnp.max(score, axis=-1, keepdims=True)
            idx = jnp.min(jnp.where(score == m, blk_id, nb), axis=-1, keepdims=True)
            sels.append(jnp.where(i > r, idx, -1))
            score = jnp.where(blk_id == idx, -jnp.inf, score)
        s = lax.dot_general(qh, k_own, nt, preferred_element_type=F32) * scale + bias_ref[hh, 0]
        s = jnp.where(row_i >= col_i, s, NEG_INF)
        m0 = jnp.max(s, axis=-1, keepdims=True)
        p = jnp.exp(s - m0)
        l0 = jnp.sum(p, axis=-1, keepdims=True)
        acc0 = jnp.dot(p.astype(BF16), v_own, preferred_element_type=F32)

        def body(j, carry, qh=qh, sels=sels, hh=hh):
            m_run, l_run, acc = carry
            j0 = pl.multiple_of(j * blk, blk)
            kb = k_ref[pl.ds(j0, blk), :]
            vb = v_ref[pl.ds(j0, blk), :]
            dist = jnp.minimum(i - j, BIAS_TILES - 1)
            hit = sels[0] == j
            for sr in sels[1:]:
                hit = hit | (sr == j)
            pen = jnp.where(hit, 0.0, NEG_INF)
            s = (lax.dot_general(qh, kb, nt, preferred_element_type=F32) * scale
                 + bias_ref[hh, dist] + pen)
            m_new = jnp.maximum(m_run, jnp.max(s, axis=-1, keepdims=True))
            alpha = jnp.exp(m_run - m_new)
            p = jnp.exp(s - m_new)
            l_new = alpha * l_run + jnp.sum(p, axis=-1, keepdims=True)
            acc_new = alpha * acc + jnp.dot(p.astype(BF16), vb, preferred_element_type=F32)
            return m_new, l_new, acc_new

        if n_slots > 0:
            _, l_fin, acc_fin = lax.fori_loop(0, i, body, (m0, l0, acc0))
        else:
            l_fin, acc_fin = l0, acc0
        outs.append(acc_fin / l_fin)

    o_ref[...] = jnp.where(lane_q < HEAD_DIM, outs[0], outs[1]).astype(o_ref.dtype)


def _moba(q, k, v, bias, batch, seq):
    t = q.shape[0]
    nb = seq // MOBA_BLOCK
    n_slots = min(MOBA_TOPK, nb - 1)
    pairs = ATTN_HEADS // 2
    kv = pl.BlockSpec((seq, LANES), lambda hp, b, i: (b, hp))
    qo = pl.BlockSpec((MOBA_BLOCK, LANES), lambda hp, b, i: (b * nb + i, hp))
    return pl.pallas_call(
        functools.partial(_moba_kernel, nb=nb, n_slots=n_slots),
        grid=(pairs, batch, nb),
        in_specs=[qo, kv, kv,
                  pl.BlockSpec((2, BIAS_TILES, MOBA_BLOCK, MOBA_BLOCK), lambda hp, b, i: (hp, 0, 0, 0))],
        out_specs=qo,
        out_shape=jax.ShapeDtypeStruct((t, ATTN_WIDTH), BF16),
        scratch_shapes=[pltpu.VMEM((nb, LANES), F32)],
        compiler_params=_cparams("arbitrary", "arbitrary", "arbitrary"),
    )(q, k, v, bias)


MERGE_TM = 256


def _merge_kernel(x_ref, ya_ref, yg_ref, ga_ref, gg_ref, gt1_ref, sc2_ref, sh2_ref,
                  pg1_ref, g2_ref, wa_ref, wg_ref, wo_ref, wq_ref, sk0_ref, sk1_ref,
                  x1_ref, h2_ref, st_ref):
    nt = (((1,), (1,)), ((), ()))
    la = jnp.dot(ya_ref[...], wa_ref[...], preferred_element_type=F32)
    lg = jnp.dot(yg_ref[...], wg_ref[...], preferred_element_type=F32)
    merged = jax.nn.sigmoid(ga_ref[...]) * la + jax.nn.sigmoid(gg_ref[...]) * lg
    y = jnp.dot(merged.astype(BF16), wo_ref[...], preferred_element_type=F32)
    x1 = x_ref[...] + gt1_ref[0] * _rms(y, pg1_ref[...])
    x1_ref[...] = x1
    h2 = _rms(x1, g2_ref[...]) * (1.0 + sc2_ref[0]) + sh2_ref[0]
    h2_ref[...] = h2
    qp = jnp.dot(h2.astype(BF16), wq_ref[...], preferred_element_type=F32).astype(BF16)
    for h in range(PEER_HEADS):
        qh = qp[:, h * PEER_QDIM:(h + 1) * PEER_QDIM]
        st_ref[2 * h] = lax.dot_general(sk0_ref[...], qh, nt, preferred_element_type=F32)
        st_ref[2 * h + 1] = lax.dot_general(sk1_ref[...], qh, nt, preferred_element_type=F32)


def _merge(x2, ya, yg, ga, gg, gt1, sc2, sh2, post_g1, pre_g2, wa, wg, wo, wq, sk0, sk1, seq):
    t = x2.shape[0]
    tm = MERGE_TM
    per_b = seq // tm
    row = lambda w: pl.BlockSpec((tm, w), lambda i: (i, 0))
    mod = pl.BlockSpec((1, 1, D_MODEL), lambda i: (i // per_b, 0, 0))
    vec = pl.BlockSpec((1, D_MODEL), lambda i: (0, 0))
    full = lambda a: pl.BlockSpec(a.shape, lambda i: (0,) * a.ndim)
    return pl.pallas_call(
        _merge_kernel,
        grid=(t // tm,),
        in_specs=[row(D_MODEL), row(ATTN_WIDTH), row(GMLP_WIDTH), row(D_MODEL), row(D_MODEL),
                  mod, mod, mod, vec, vec, full(wa), full(wg), full(wo), full(wq), full(sk0), full(sk1)],
        out_specs=[row(D_MODEL), row(D_MODEL),
                   pl.BlockSpec((2 * PEER_HEADS, PEER_NKEYS, tm), lambda i: (0, 0, i))],
        out_shape=[jax.ShapeDtypeStruct((t, D_MODEL), F32), jax.ShapeDtypeStruct((t, D_MODEL), F32),
                   jax.ShapeDtypeStruct((2 * PEER_HEADS, PEER_NKEYS, t), F32)],
        compiler_params=_cparams("arbitrary"),
    )(x2, ya, yg, ga, gg, gt1, sc2, sh2, post_g1, pre_g2, wa, wg, wo, wq, sk0, sk1)


ROUTE_SUB = 8
_CANDS = [(a, b) for a in range(PEER_TOPK) for b in range(PEER_TOPK) if (a + 1) * (b + 1) <= PEER_TOPK]


def _route_kernel(s_ref, e_ref, g_ref, buf):
    kio = lax.broadcasted_iota(I32, (PEER_NKEYS, ROUTE_SUB, LANES), 0)

    def top16(half):
        buf[...] = s_ref[half]
        vals, idxs = [], []
        for _ in range(PEER_TOPK):
            s = buf[...]
            m = jnp.max(s, axis=0)
            idx = jnp.min(jnp.where(s == m[None], kio, PEER_NKEYS), axis=0)
            buf[...] = jnp.where(kio == idx[None], -jnp.inf, s)
            vals.append(m)
            idxs.append(idx)
        return vals, idxs

    v1, i1 = top16(0)
    v2, i2 = top16(1)
    cand = [v1[a] + v2[b] for a, b in _CANDS]
    eid = [i1[a] * PEER_NKEYS + i2[b] for a, b in _CANDS]
    flat = [a * PEER_TOPK + b for a, b in _CANDS]
    sc, es = [], []
    for _ in range(PEER_TOPK):
        m = functools.reduce(jnp.maximum, cand)
        idx = functools.reduce(jnp.minimum,
                               [jnp.where(c == m, f, PEER_TOPK * PEER_TOPK) for c, f in zip(cand, flat)])
        es.append(functools.reduce(jnp.maximum, [jnp.where(idx == f, e, -1) for e, f in zip(eid, flat)]))
        cand = [jnp.where(idx == f, -jnp.inf, c) for c, f in zip(cand, flat)]
        sc.append(m)
    ex = [jnp.exp(s - sc[0]) for s in sc]
    tot = functools.reduce(jnp.add, ex)
    for r in range(PEER_TOPK):
        e_ref[0, r] = es[r]
        g_ref[0, r] = ex[r] / tot


def _route(st4):
    ntile = st4.shape[2]
    grp = ROUTE_SUB
    out = pl.BlockSpec((1, PEER_TOPK, grp, LANES), lambda h, g: (h, 0, g, 0))
    shp = (PEER_HEADS, PEER_TOPK, ntile, LANES)
    return pl.pallas_call(
        _route_kernel,
        grid=(PEER_HEADS, ntile // grp),
        in_specs=[pl.BlockSpec((2, PEER_NKEYS, grp, LANES), lambda h, g: (h, 0, g, 0))],
        out_specs=[out, out],
        out_shape=[jax.ShapeDtypeStruct(shp, I32), jax.ShapeDtypeStruct(shp, F32)],
        scratch_shapes=[pltpu.VMEM((PEER_NKEYS, grp, LANES), F32)],
        compiler_params=_cparams("arbitrary", "arbitrary"),
    )(st4)


PEER_TT = 128
PEER_NBUF = 8
SUBLANES = 8
PEER_KG = PEER_PICKS // SUBLANES
PEER_RT = D_MODEL // LANES


def _pack_rows(u, v):
    ub = lax.bitcast_convert_type(u.astype(BF16), jnp.uint16).astype(jnp.uint32)
    vb = lax.bitcast_convert_type(v.astype(BF16), jnp.uint16).astype(jnp.uint32)
    return (ub | (vb << 16)).reshape(u.shape[0], PEER_RT, LANES)


def _peer_kernel(e_ref, h2_ref, gt_ref, x1_ref, gt2_ref, pg2_ref, uv_ref, o_ref, *scratch):
    bufs = scratch[:PEER_NBUF]
    sem, y_sc = scratch[PEER_NBUF:]
    tt = PEER_TT
    ahead = PEER_NBUF - 1

    def issue(t, slot):
        for k in range(PEER_PICKS):
            pltpu.make_async_copy(uv_ref.at[e_ref[k, t]],
                                  bufs[slot].at[k // SUBLANES, :, k % SUBLANES], sem.at[slot]).start()

    def wait(slot):
        pltpu.make_async_copy(bufs[slot], bufs[slot], sem.at[slot]).wait()

    lane_t = lax.broadcasted_iota(I32, (PEER_PICKS, tt), 1)
    himask = jnp.uint32(0xFFFF0000)

    def compute(t, slot):
        buf = bufs[slot]
        xb = h2_ref[pl.ds(t, 1), :]
        xr = [xb[:, r * LANES:(r + 1) * LANES] for r in range(PEER_RT)]
        parts = []
        for kg in range(PEER_KG):
            acc = None
            for r in range(PEER_RT):
                term = lax.bitcast_convert_type(buf[kg, r] << 16, F32) * xr[r]
                acc = term if acc is None else acc + term
            parts.append(acc)
        a = jnp.sum(jnp.concatenate(parts, axis=0), axis=-1, keepdims=True)
        gcol = jnp.sum(jnp.where(lane_t == t, gt_ref[...], 0.0), axis=-1, keepdims=True)
        wb = jnp.broadcast_to(gcol * jax.nn.gelu(a), (PEER_PICKS, LANES))
        cols = []
        for r in range(PEER_RT):
            acc = None
            for kg in range(PEER_KG):
                term = (wb[kg * SUBLANES:(kg + 1) * SUBLANES]
                        * lax.bitcast_convert_type(buf[kg, r] & himask, F32))
                acc = term if acc is None else acc + term
            cols.append(jnp.sum(acc, axis=0, keepdims=True))
        y_sc[pl.ds(t, 1), :] = jnp.concatenate(cols, axis=-1)

    for t in range(ahead):
        issue(t, t)

    def group(g, c):
        for s in range(PEER_NBUF):
            t = g * PEER_NBUF + s
            issue(t + ahead, (s + ahead) % PEER_NBUF)
            wait(s)
            compute(t, s)
        return c

    ngroups = tt // PEER_NBUF
    lax.fori_loop(0, ngroups - 1, group, 0)
    for s in range(PEER_NBUF):
        t = (ngroups - 1) * PEER_NBUF + s
        if s == 0:
            issue(t + ahead, ahead)
        wait(s)
        compute(t, s)
    o_ref[...] = x1_ref[...] + gt2_ref[0] * _rms(y_sc[...], pg2_ref[...])


def _peer(e_t, h2, g_t, x1, gt2, post_g2, uv, seq):
    t = h2.shape[0]
    tt = PEER_TT
    per_b = seq // tt
    row = pl.BlockSpec((tt, D_MODEL), lambda i: (i, 0))
    return pl.pallas_call(
        _peer_kernel,
        grid=(t // tt,),
        in_specs=[pl.BlockSpec((PEER_PICKS, tt), lambda i: (0, i), memory_space=pltpu.SMEM),
                  row,
                  pl.BlockSpec((PEER_PICKS, tt), lambda i: (0, i)),
                  row,
                  pl.BlockSpec((1, 1, D_MODEL), lambda i: (i // per_b, 0, 0)),
                  pl.BlockSpec((1, D_MODEL), lambda i: (0, 0)),
                  pl.BlockSpec(memory_space=pl.ANY)],
        out_specs=row,
        out_shape=jax.ShapeDtypeStruct((t, D_MODEL), F32),
        scratch_shapes=[pltpu.VMEM((PEER_KG, PEER_RT, SUBLANES, LANES), jnp.uint32) for _ in range(PEER_NBUF)]
        + [pltpu.SemaphoreType.DMA((PEER_NBUF,)), pltpu.VMEM((tt, D_MODEL), F32)],
        compiler_params=_cparams("arbitrary"),
    )(e_t, h2, g_t, x1, gt2, post_g2, uv)


def kernel(x, c, ada_w, ada_b, pre_g1, post_g1, w_in, gmlp_ln_g, gmlp_ln_b, gmlp_w_s, gmlp_b_s,
           rel_bias, w_branch_attn, w_branch_gmlp, w_out, pre_g2, post_g2, peer_w_query,
           peer_sub_keys, peer_u, peer_v):
    batch, seq, d = x.shape
    depth = ada_w.shape[0]
    t = batch * seq
    assert d == D_MODEL and batch <= 8 and seq % (2 * MOBA_BLOCK) == 0 and t % (ROUTE_SUB * LANES) == 0

    c8 = jnp.zeros((8, d), F32).at[:batch].set(c)
    bias = _bias_tiles(rel_bias)
    x2 = x.reshape(t, d)
    for l in range(depth):
        mod = _ada(c8, ada_w[l], ada_b[l][None, :])[:batch]
        sh1, sc1, gt1, sh2, sc2, gt2 = [m[:, None, :] for m in jnp.split(mod, 6, axis=-1)]

        q, k, v, zu, zv, ga, gg = _inproj(x2, sc1, sh1, pre_g1[l][None, :], w_in[l].astype(BF16), seq)
        yg = _gmlp(zu, zv, gmlp_ln_g[l][None, :], gmlp_ln_b[l][None, :], gmlp_w_s[l], gmlp_b_s[l].T)
        ya = _moba(q, k, v, bias, batch, seq)

        zpad = jnp.zeros((PEER_NKEYS, PEER_HALF), F32)
        sk0 = jnp.concatenate([peer_sub_keys[l, 0], zpad], axis=1).astype(BF16)
        sk1 = jnp.concatenate([zpad, peer_sub_keys[l, 1]], axis=1).astype(BF16)
        x1, h2, st = _merge(x2, ya, yg, ga, gg, gt1, sc2, sh2, post_g1[l][None, :], pre_g2[l][None, :],
                            w_branch_attn[l].astype(BF16), w_branch_gmlp[l].astype(BF16),
                            w_out[l].astype(BF16), peer_w_query[l].astype(BF16), sk0, sk1, seq)

        e4, g4 = _route(st.reshape(2 * PEER_HEADS, PEER_NKEYS, t // LANES, LANES))
        e_t = e4.reshape(PEER_PICKS, t)
        g_t = g4.reshape(PEER_PICKS, t)
        uv = _pack_rows(peer_u[l], peer_v[l])
        x2 = _peer(e_t, h2, g_t, x1, gt2, post_g2[l][None, :], uv, seq)
    return x2.reshape(batch, seq, d)
```

```python
import functools
import math

import numpy as np
import jax
import jax.numpy as jnp
from jax import lax
from jax.experimental import pallas as pl
from jax.experimental.pallas import tpu as pltpu

F32 = jnp.float32
BF16 = jnp.bfloat16
I32 = jnp.int32

D_MODEL = 1024
ATTN_HEADS = 8
HEAD_DIM = 64
ATTN_WIDTH = ATTN_HEADS * HEAD_DIM
MOBA_BLOCK = 256
MOBA_TOPK = 3
REL_BUCKETS = 32
REL_MAX_DISTANCE = 2048
GMLP_WIDTH = D_MODEL // 2
GMLP_GROUPS = 8
GMLP_GROUP_DIM = GMLP_WIDTH // GMLP_GROUPS
GMLP_CHUNK = 128
PEER_HEADS = 8
PEER_NKEYS = 128
PEER_QDIM = 128
PEER_HALF = PEER_QDIM // 2
PEER_TOPK = 16
PEER_PICKS = PEER_HEADS * PEER_TOPK
IN_WIDTH = 3 * ATTN_WIDTH + 2 * GMLP_WIDTH + 2 * D_MODEL
NORM_EPS = 1e-6
NEG_INF = -1e30

LANES = 128
BIAS_TILES = 10
VMEM_LIMIT = 56 * 1024 * 1024


def _cparams(*sem):
    return pltpu.CompilerParams(dimension_semantics=sem, vmem_limit_bytes=VMEM_LIMIT)


def _rms(xf, g):
    return xf * lax.rsqrt(jnp.mean(xf * xf, axis=-1, keepdims=True) + NORM_EPS) * g


def _ada_kernel(c_ref, w_ref, b_ref, o_ref):
    c = c_ref[...]
    cond = c * jax.nn.sigmoid(c)
    o_ref[...] = jnp.dot(cond, w_ref[...], preferred_element_type=F32,
                         precision=lax.Precision.HIGHEST) + b_ref[...]


def _ada(c8, w, b):
    n = w.shape[1]
    tn = 768
    return pl.pallas_call(
        _ada_kernel,
        grid=(n // tn,),
        in_specs=[pl.BlockSpec((8, D_MODEL), lambda j: (0, 0)),
                  pl.BlockSpec((D_MODEL, tn), lambda j: (0, j)),
                  pl.BlockSpec((1, tn), lambda j: (0, j))],
        out_specs=pl.BlockSpec((8, tn), lambda j: (0, j)),
        out_shape=jax.ShapeDtypeStruct((8, n), F32),
        compiler_params=_cparams("arbitrary"),
    )(c8, w, b)


def _bucket_table():
    max_exact = REL_BUCKETS // 2
    d = np.arange(BIAS_TILES)[:, None, None]
    c = np.arange(MOBA_BLOCK)[None, :, None]
    r = np.arange(MOBA_BLOCK)[None, None, :]
    n = np.maximum(d * MOBA_BLOCK + r - c, 0)
    nf = np.maximum(n, max_exact).astype(np.float32)
    large = max_exact + (np.log(nf / np.float32(max_exact)) / np.float32(math.log(REL_MAX_DISTANCE / max_exact))
                         * np.float32(REL_BUCKETS - max_exact)).astype(np.int32)
    large = np.minimum(large, REL_BUCKETS - 1)
    return np.where(n < max_exact, n, large).astype(np.int32)


def _bias_kernel(rb_ref, bucket_ref, o_ref):
    h = pl.program_id(0)
    for d in range(BIAS_TILES):
        bk = bucket_ref[d]
        acc = jnp.zeros(bk.shape, F32)
        for b in range(REL_BUCKETS):
            acc = jnp.where(bk == b, rb_ref[b, h], acc)
        if d == 0:
            key = lax.broadcasted_iota(I32, bk.shape, 0)
            qry = lax.broadcasted_iota(I32, bk.shape, 1)
            acc = jnp.where(qry >= key, acc, NEG_INF)
        o_ref[0, d] = acc


def _bias_tiles(rel_bias):
    bucket = jnp.asarray(_bucket_table())
    blk = (BIAS_TILES, MOBA_BLOCK, MOBA_BLOCK)
    return pl.pallas_call(
        _bias_kernel,
        grid=(ATTN_HEADS,),
        in_specs=[pl.BlockSpec(memory_space=pltpu.SMEM),
                  pl.BlockSpec(blk, lambda h: (0, 0, 0))],
        out_specs=pl.BlockSpec((1,) + blk, lambda h: (h, 0, 0, 0)),
        out_shape=jax.ShapeDtypeStruct((ATTN_HEADS,) + blk, F32),
        compiler_params=_cparams("arbitrary"),
    )(rel_bias, bucket)


IN_TM = MOBA_BLOCK
IN_NCHUNK = 512


def _inproj_kernel(x_ref, sc_ref, sh_ref, g_ref, w_ref, wvt_ref,
                   q_ref, k_ref, vt_ref, zu_ref, zv_ref, ga_ref, gg_ref):
    x = x_ref[...]
    h = _rms(x, g_ref[...]) * (1.0 + sc_ref[0]) + sh_ref[0]
    hb = h.astype(BF16)
    nt = (((1,), (1,)), ((), ()))
    vt_ref[0, 0] = lax.dot_general(wvt_ref[...], hb, nt, preferred_element_type=F32).astype(vt_ref.dtype)
    outs = ((q_ref, ATTN_WIDTH), (k_ref, ATTN_WIDTH), (None, ATTN_WIDTH),
            (zu_ref, GMLP_WIDTH), (zv_ref, GMLP_WIDTH), (ga_ref, D_MODEL), (gg_ref, D_MODEL))
    off = 0
    for ref, width in outs:
        for c0 in range(0, width if ref is not None else 0, IN_NCHUNK):
            p = jnp.dot(hb, w_ref[:, off + c0:off + c0 + IN_NCHUNK], preferred_element_type=F32)
            ref[:, c0:c0 + IN_NCHUNK] = p.astype(ref.dtype)
        off += width


def _inproj(x2, sc1, sh1, pre_g, w_in_bf, wvt_bf, batch, seq):
    t = x2.shape[0]
    tm = IN_TM
    per_b = seq // tm
    row = lambda w: pl.BlockSpec((tm, w), lambda i: (i, 0))
    mod = pl.BlockSpec((1, 1, D_MODEL), lambda i: (i // per_b, 0, 0))
    return pl.pallas_call(
        _inproj_kernel,
        grid=(t // tm,),
        in_specs=[row(D_MODEL), mod, mod,
                  pl.BlockSpec((1, D_MODEL), lambda i: (0, 0)),
                  pl.BlockSpec((D_MODEL, IN_WIDTH), lambda i: (0, 0)),
                  pl.BlockSpec((ATTN_WIDTH, D_MODEL), lambda i: (0, 0))],
        out_specs=[row(ATTN_WIDTH), row(ATTN_WIDTH),
                   pl.BlockSpec((1, 1, ATTN_WIDTH, tm), lambda i: (i // per_b, i % per_b, 0, 0)),
                   row(GMLP_WIDTH), row(GMLP_WIDTH), row(D_MODEL), row(D_MODEL)],
        out_shape=[jax.ShapeDtypeStruct((t, ATTN_WIDTH), BF16)] * 2
        + [jax.ShapeDtypeStruct((batch, per_b, ATTN_WIDTH, tm), BF16)]
        + [jax.ShapeDtypeStruct((t, GMLP_WIDTH), F32)] * 2
        + [jax.ShapeDtypeStruct((t, D_MODEL), F32)] * 2,
        compiler_params=_cparams("arbitrary"),
    )(x2, sc1, sh1, pre_g, w_in_bf, wvt_bf)


GMLP_TM = 512


def _gmlp_kernel(zu_ref, zv_ref, lg_ref, lb_ref, ws_ref, bs_ref, y_ref):
    u = jax.nn.gelu(zu_ref[...])
    gv = jax.nn.gelu(zv_ref[...])
    mu = jnp.mean(gv, axis=-1, keepdims=True)
    vc = gv - mu
    var = jnp.mean(vc * vc, axis=-1, keepdims=True)
    v = (vc * lax.rsqrt(var + NORM_EPS) * lg_ref[...] + lb_ref[...]).astype(BF16)
    tri = (lax.broadcasted_iota(I32, (GMLP_CHUNK, GMLP_CHUNK), 0)
           >= lax.broadcasted_iota(I32, (GMLP_CHUNK, GMLP_CHUNK), 1))
    lane = lax.broadcasted_iota(I32, (GMLP_CHUNK, LANES), 1)
    lo = lane < GMLP_GROUP_DIM
    zero = jnp.zeros((), BF16)
    for pair in range(GMLP_GROUPS // 2):
        w0 = jnp.where(tri, ws_ref[2 * pair], 0.0).astype(BF16)
        w1 = jnp.where(tri, ws_ref[2 * pair + 1], 0.0).astype(BF16)
        b0 = bs_ref[:, 2 * pair:2 * pair + 1]
        b1 = bs_ref[:, 2 * pair + 1:2 * pair + 2]
        bias = jnp.where(lo, b0, b1)
        cols = slice(pair * LANES, (pair + 1) * LANES)
        for ch in range(GMLP_TM // GMLP_CHUNK):
            rows = slice(ch * GMLP_CHUNK, (ch + 1) * GMLP_CHUNK)
            v2 = v[rows, cols]
            mixed = (jnp.dot(w0, jnp.where(lo, v2, zero), preferred_element_type=F32)
                     + jnp.dot(w1, jnp.where(lo, zero, v2), preferred_element_type=F32) + bias)
            y_ref[rows, cols] = (u[rows, cols] * mixed).astype(y_ref.dtype)


def _gmlp(zu, zv, ln_g, ln_b, w_s, bs_t):
    t = zu.shape[0]
    tm = GMLP_TM
    row = pl.BlockSpec((tm, GMLP_WIDTH), lambda i: (i, 0))
    vec = pl.BlockSpec((1, GMLP_WIDTH), lambda i: (0, 0))
    return pl.pallas_call(
        _gmlp_kernel,
        grid=(t // tm,),
        in_specs=[row, row, vec, vec,
                  pl.BlockSpec((GMLP_GROUPS, GMLP_CHUNK, GMLP_CHUNK), lambda i: (0, 0, 0)),
                  pl.BlockSpec((GMLP_CHUNK, GMLP_GROUPS), lambda i: (0, 0))],
        out_specs=row,
        out_shape=jax.ShapeDtypeStruct((t, GMLP_WIDTH), BF16),
        compiler_params=_cparams("arbitrary"),
    )(zu, zv, ln_g, ln_b, w_s, bs_t)


def _moba_kernel(q_ref, k_ref, vt_ref, bias_ref, o_ref, kmean_sc, *, nb, n_slots):
    i = pl.program_id(2)
    blk = MOBA_BLOCK
    scale = HEAD_DIM ** -0.5
    nt = (((1,), (1,)), ((), ()))

    @pl.when(i == 0)
    def _():
        def body(j, c):
            kb = k_ref[pl.ds(pl.multiple_of(j * blk, blk), blk), :].astype(F32)
            kmean_sc[pl.ds(j, 1), :] = jnp.mean(kb, axis=0, keepdims=True)
            return c
        lax.fori_loop(0, nb, body, 0)

    q2 = q_ref[...]
    km = kmean_sc[...].astype(BF16)
    lane_q = lax.broadcasted_iota(I32, (blk, LANES), 1)
    blk_id = lax.broadcasted_iota(I32, (nb, blk), 0).astype(F32)
    i_f = i.astype(F32)
    zero = jnp.zeros((), BF16)

    qs, sels = [], []
    for hh in range(2):
        head_lanes = (lane_q < HEAD_DIM) if hh == 0 else (lane_q >= HEAD_DIM)
        qh = jnp.where(head_lanes, q2, zero)
        qs.append((qh.astype(F32) * scale).astype(BF16))
        score = lax.dot_general(km, qh, nt, preferred_element_type=F32)
        score = jnp.where(blk_id < i_f, score, NEG_INF)
        sel = []
        for r in range(n_slots):
            m = jnp.max(score, axis=0, keepdims=True)
            idx = jnp.min(jnp.where(score == m, blk_id, float(nb)), axis=0, keepdims=True)
            sel.append(jnp.where(i > r, idx, -1.0))
            score = jnp.where(blk_id == idx, -jnp.inf, score)
        sels.append(sel)

    def tile(jb, hh, m_run):
        jb_f = jb.astype(F32)
        kb = k_ref[pl.ds(pl.multiple_of(jb * blk, blk), blk), :]
        dist = jnp.clip(i - jb, 0, BIAS_TILES - 1)
        hit = jb_f == i_f
        for sr in sels[hh]:
            hit = hit | (sr == jb_f)
        s = lax.dot_general(kb, qs[hh], nt, preferred_element_type=F32) + bias_ref[hh, dist]
        cmax = jnp.where(hit, jnp.max(s, axis=0, keepdims=True), NEG_INF)
        return s, hit, jnp.maximum(m_run, cmax)

    def body(step, carry):
        pr = i // 2 - step
        new = []
        for hh in range(2):
            m_run, l_run, acc = carry[hh]
            s0, hit0, m_new = tile(2 * pr, hh, m_run)
            s1, hit1, m_new = tile(2 * pr + 1, hh, m_new)
            alpha = jnp.exp(m_run - m_new)
            p0 = jnp.exp(s0 - jnp.where(hit0, m_new, m_new - NEG_INF))
            p1 = jnp.exp(s1 - jnp.where(hit1, m_new, m_new - NEG_INF))
            l_new = alpha * l_run + jnp.sum(p0, axis=0, keepdims=True) + jnp.sum(p1, axis=0, keepdims=True)
            acc_new = (alpha * acc
                       + jnp.dot(vt_ref[0, 2 * pr], p0.astype(BF16), preferred_element_type=F32)
                       + jnp.dot(vt_ref[0, 2 * pr + 1], p1.astype(BF16), preferred_element_type=F32))
            new.append((m_new, l_new, acc_new))
        return tuple(new)

    init = tuple((jnp.full((1, blk), NEG_INF, F32), jnp.zeros((1, blk), F32), jnp.zeros((LANES, blk), F32))
                 for _ in range(2))
    fin = lax.fori_loop(0, i // 2 + 1, body, init)
    o0 = fin[0][2] / fin[0][1]
    o1 = fin[1][2] / fin[1][1]
    out_t = jnp.concatenate([o0[:HEAD_DIM], o1[HEAD_DIM:]], axis=0)
    o_ref[...] = out_t.T.astype(o_ref.dtype)


def _moba(q, k, vt, bias, batch, seq):
    t = q.shape[0]
    nb = seq // MOBA_BLOCK
    n_slots = min(MOBA_TOPK, nb - 1)
    pairs = ATTN_HEADS // 2
    qo = pl.BlockSpec((MOBA_BLOCK, LANES), lambda hp, b, i: (b * nb + i, hp))
    return pl.pallas_call(
        functools.partial(_moba_kernel, nb=nb, n_slots=n_slots),
        grid=(pairs, batch, nb),
        in_specs=[qo,
                  pl.BlockSpec((seq, LANES), lambda hp, b, i: (b, hp)),
                  pl.BlockSpec((1, nb, LANES, MOBA_BLOCK), lambda hp, b, i: (b, 0, hp, 0)),
                  pl.BlockSpec((2, BIAS_TILES, MOBA_BLOCK, MOBA_BLOCK), lambda hp, b, i: (hp, 0, 0, 0))],
        out_specs=qo,
        out_shape=jax.ShapeDtypeStruct((t, ATTN_WIDTH), BF16),
        scratch_shapes=[pltpu.VMEM((nb, LANES), F32)],
        compiler_params=_cparams("arbitrary", "arbitrary", "arbitrary"),
    )(q, k, vt, bias)


MERGE_TM = 256


def _merge_kernel(x_ref, ya_ref, yg_ref, ga_ref, gg_ref, gt1_ref, sc2_ref, sh2_ref,
                  pg1_ref, g2_ref, wa_ref, wg_ref, wo_ref, wq_ref, sk0_ref, sk1_ref,
                  x1_ref, h2_ref, st_ref):
    nt = (((1,), (1,)), ((), ()))
    la = jnp.dot(ya_ref[...], wa_ref[...], preferred_element_type=F32)
    lg = jnp.dot(yg_ref[...], wg_ref[...], preferred_element_type=F32)
    merged = jax.nn.sigmoid(ga_ref[...]) * la + jax.nn.sigmoid(gg_ref[...]) * lg
    y = jnp.dot(merged.astype(BF16), wo_ref[...], preferred_element_type=F32)
    x1 = x_ref[...] + gt1_ref[0] * _rms(y, pg1_ref[...])
    x1_ref[...] = x1
    h2 = _rms(x1, g2_ref[...]) * (1.0 + sc2_ref[0]) + sh2_ref[0]
    h2_ref[...] = h2
    qp = jnp.dot(h2.astype(BF16), wq_ref[...], preferred_element_type=F32).astype(BF16)
    for h in range(PEER_HEADS):
        qh = qp[:, h * PEER_QDIM:(h + 1) * PEER_QDIM]
        st_ref[2 * h] = lax.dot_general(sk0_ref[...], qh, nt, preferred_element_type=F32)
        st_ref[2 * h + 1] = lax.dot_general(sk1_ref[...], qh, nt, preferred_element_type=F32)


def _merge(x2, ya, yg, ga, gg, gt1, sc2, sh2, post_g1, pre_g2, wa, wg, wo, wq, sk0, sk1, seq):
    t = x2.shape[0]
    tm = MERGE_TM
    per_b = seq // tm
    row = lambda w: pl.BlockSpec((tm, w), lambda i: (i, 0))
    mod = pl.BlockSpec((1, 1, D_MODEL), lambda i: (i // per_b, 0, 0))
    vec = pl.BlockSpec((1, D_MODEL), lambda i: (0, 0))
    full = lambda a: pl.BlockSpec(a.shape, lambda i: (0,) * a.ndim)
    return pl.pallas_call(
        _merge_kernel,
        grid=(t // tm,),
        in_specs=[row(D_MODEL), row(ATTN_WIDTH), row(GMLP_WIDTH), row(D_MODEL), row(D_MODEL),
                  mod, mod, mod, vec, vec, full(wa), full(wg), full(wo), full(wq), full(sk0), full(sk1)],
        out_specs=[row(D_MODEL), row(D_MODEL),
                   pl.BlockSpec((2 * PEER_HEADS, PEER_NKEYS, tm), lambda i: (0, 0, i))],
        out_shape=[jax.ShapeDtypeStruct((t, D_MODEL), F32), jax.ShapeDtypeStruct((t, D_MODEL), F32),
                   jax.ShapeDtypeStruct((2 * PEER_HEADS, PEER_NKEYS, t), F32)],
        compiler_params=_cparams("arbitrary"),
    )(x2, ya, yg, ga, gg, gt1, sc2, sh2, post_g1, pre_g2, wa, wg, wo, wq, sk0, sk1)


ROUTE_SUB = 8
_CANDS = [(a, b) for a in range(PEER_TOPK) for b in range(PEER_TOPK) if (a + 1) * (b + 1) <= PEER_TOPK]


def _route_kernel(s_ref, e_ref, g_ref, buf):
    kio = lax.broadcasted_iota(I32, (PEER_NKEYS, ROUTE_SUB, LANES), 0)

    def top16(half):
        buf[...] = s_ref[half]
        vals, idxs = [], []
        for _ in range(PEER_TOPK):
            s = buf[...]
            m = jnp.max(s, axis=0)
            idx = jnp.min(jnp.where(s == m[None], kio, PEER_NKEYS), axis=0)
            buf[...] = jnp.where(kio == idx[None], -jnp.inf, s)
            vals.append(m)
            idxs.append(idx)
        return vals, idxs

    v1, i1 = top16(0)
    v2, i2 = top16(1)
    cand = [v1[a] + v2[b] for a, b in _CANDS]
    eid = [i1[a] * PEER_NKEYS + i2[b] for a, b in _CANDS]
    flat = [a * PEER_TOPK + b for a, b in _CANDS]
    sc, es = [], []
    for _ in range(PEER_TOPK):
        m = functools.reduce(jnp.maximum, cand)
        idx = functools.reduce(jnp.minimum,
                               [jnp.where(c == m, f, PEER_TOPK * PEER_TOPK) for c, f in zip(cand, flat)])
        es.append(functools.reduce(jnp.maximum, [jnp.where(idx == f, e, -1) for e, f in zip(eid, flat)]))
        cand = [jnp.where(idx == f, -jnp.inf, c) for c, f in zip(cand, flat)]
        sc.append(m)
    ex = [jnp.exp(s - sc[0]) for s in sc]
    tot = functools.reduce(jnp.add, ex)
    for r in range(PEER_TOPK):
        e_ref[0, r] = es[r]
        g_ref[0, r] = ex[r] / tot


def _route(st4):
    ntile = st4.shape[2]
    grp = ROUTE_SUB
    out = pl.BlockSpec((1, PEER_TOPK, grp, LANES), lambda h, g: (h, 0, g, 0))
    shp = (PEER_HEADS, PEER_TOPK, ntile, LANES)
    return pl.pallas_call(
        _route_kernel,
        grid=(PEER_HEADS, ntile // grp),
        in_specs=[pl.BlockSpec((2, PEER_NKEYS, grp, LANES), lambda h, g: (h, 0, g, 0))],
        out_specs=[out, out],
        out_shape=[jax.ShapeDtypeStruct(shp, I32), jax.ShapeDtypeStruct(shp, F32)],
        scratch_shapes=[pltpu.VMEM((PEER_NKEYS, grp, LANES), F32)],
        compiler_params=_cparams("arbitrary", "arbitrary"),
    )(st4)


PEER_TT = 128
PEER_NBUF = 8
SUBLANES = 8
PEER_KG = PEER_PICKS // SUBLANES
PEER_RT = D_MODEL // LANES


def _pack_rows(u, v):
    ub = lax.bitcast_convert_type(u.astype(BF16), jnp.uint16).astype(jnp.uint32)
    vb = lax.bitcast_convert_type(v.astype(BF16), jnp.uint16).astype(jnp.uint32)
    return (ub | (vb << 16)).reshape(u.shape[0], PEER_RT, LANES)


def _peer_kernel(e_ref, h2_ref, gt_ref, x1_ref, gt2_ref, pg2_ref, uv_ref, o_ref, *scratch):
    bufs = scratch[:PEER_NBUF]
    sem, y_sc = scratch[PEER_NBUF:]
    tt = PEER_TT
    ahead = PEER_NBUF - 1

    def issue(t, slot):
        for k in range(PEER_PICKS):
            pltpu.make_async_copy(uv_ref.at[e_ref[k, t]],
                                  bufs[slot].at[k // SUBLANES, :, k % SUBLANES], sem.at[slot]
                                  ).start(priority=k % 2)

    def wait(slot):
        pltpu.make_async_copy(bufs[slot], bufs[slot], sem.at[slot]).wait()

    lane_t = lax.broadcasted_iota(I32, (PEER_PICKS, tt), 1)
    himask = jnp.uint32(0xFFFF0000)

    def compute(t, slot):
        buf = bufs[slot]
        xb = h2_ref[pl.ds(t, 1), :]
        xr = [xb[:, r * LANES:(r + 1) * LANES] for r in range(PEER_RT)]
        parts = []
        for kg in range(PEER_KG):
            acc = None
            for r in range(PEER_RT):
                term = lax.bitcast_convert_type(buf[kg, r] << 16, F32) * xr[r]
                acc = term if acc is None else acc + term
            parts.append(acc)
        a = jnp.sum(jnp.concatenate(parts, axis=0), axis=-1, keepdims=True)
        gcol = jnp.sum(jnp.where(lane_t == t, gt_ref[...], 0.0), axis=-1, keepdims=True)
        wb = jnp.broadcast_to(gcol * jax.nn.gelu(a), (PEER_PICKS, LANES))
        cols = []
        for r in range(PEER_RT):
            acc = None
            for kg in range(PEER_KG):
                term = (wb[kg * SUBLANES:(kg + 1) * SUBLANES]
                        * lax.bitcast_convert_type(buf[kg, r] & himask, F32))
                acc = term if acc is None else acc + term
            cols.append(jnp.sum(acc, axis=0, keepdims=True))
        y_sc[pl.ds(t, 1), :] = jnp.concatenate(cols, axis=-1)

    for t in range(ahead):
        issue(t, t)

    def group(g, c):
        for s in range(PEER_NBUF):
            t = g * PEER_NBUF + s
            issue(t + ahead, (s + ahead) % PEER_NBUF)
            wait(s)
            compute(t, s)
        return c

    ngroups = tt // PEER_NBUF
    lax.fori_loop(0, ngroups - 1, group, 0)
    for s in range(PEER_NBUF):
        t = (ngroups - 1) * PEER_NBUF + s
        if s == 0:
            issue(t + ahead, ahead)
        wait(s)
        compute(t, s)
    o_ref[...] = x1_ref[...] + gt2_ref[0] * _rms(y_sc[...], pg2_ref[...])


def _peer(e_t, h2, g_t, x1, gt2, post_g2, uv, seq):
    t = h2.shape[0]
    tt = PEER_TT
    per_b = seq // tt
    row = pl.BlockSpec((tt, D_MODEL), lambda i: (i, 0))
    return pl.pallas_call(
        _peer_kernel,
        grid=(t // tt,),
        in_specs=[pl.BlockSpec((PEER_PICKS, tt), lambda i: (0, i), memory_space=pltpu.SMEM),
                  row,
                  pl.BlockSpec((PEER_PICKS, tt), lambda i: (0, i)),
                  row,
                  pl.BlockSpec((1, 1, D_MODEL), lambda i: (i // per_b, 0, 0)),
                  pl.BlockSpec((1, D_MODEL), lambda i: (0, 0)),
                  pl.BlockSpec(memory_space=pl.ANY)],
        out_specs=row,
        out_shape=jax.ShapeDtypeStruct((t, D_MODEL), F32),
        scratch_shapes=[pltpu.VMEM((PEER_KG, PEER_RT, SUBLANES, LANES), jnp.uint32) for _ in range(PEER_NBUF)]
        + [pltpu.SemaphoreType.DMA((PEER_NBUF,)), pltpu.VMEM((tt, D_MODEL), F32)],
        compiler_params=_cparams("arbitrary"),
    )(e_t, h2, g_t, x1, gt2, post_g2, uv)


def kernel(x, c, ada_w, ada_b, pre_g1, post_g1, w_in, gmlp_ln_g, gmlp_ln_b, gmlp_w_s, gmlp_b_s,
           rel_bias, w_branch_attn, w_branch_gmlp, w_out, pre_g2, post_g2, peer_w_query,
           peer_sub_keys, peer_u, peer_v):
    batch, seq, d = x.shape
    depth = ada_w.shape[0]
    t = batch * seq
    assert d == D_MODEL and batch <= 8 and seq % (2 * MOBA_BLOCK) == 0 and t % (ROUTE_SUB * LANES) == 0

    c8 = jnp.zeros((8, d), F32).at[:batch].set(c)
    bias = _bias_tiles(rel_bias)
    x2 = x.reshape(t, d)
    for l in range(depth):
        mod = _ada(c8, ada_w[l], ada_b[l][None, :])[:batch]
        sh1, sc1, gt1, sh2, sc2, gt2 = [m[:, None, :] for m in jnp.split(mod, 6, axis=-1)]

        w_in_bf = w_in[l].astype(BF16)
        wvt = w_in_bf[:, 2 * ATTN_WIDTH:3 * ATTN_WIDTH].T
        q, k, v, zu, zv, ga, gg = _inproj(x2, sc1, sh1, pre_g1[l][None, :], w_in_bf, wvt, batch, seq)
        yg = _gmlp(zu, zv, gmlp_ln_g[l][None, :], gmlp_ln_b[l][None, :], gmlp_w_s[l], gmlp_b_s[l].T)
        ya = _moba(q, k, v, bias, batch, seq)

        zpad = jnp.zeros((PEER_NKEYS, PEER_HALF), F32)
        sk0 = jnp.concatenate([peer_sub_keys[l, 0], zpad], axis=1).astype(BF16)
        sk1 = jnp.concatenate([zpad, peer_sub_keys[l, 1]], axis=1).astype(BF16)
        x1, h2, st = _merge(x2, ya, yg, ga, gg, gt1, sc2, sh2, post_g1[l][None, :], pre_g2[l][None, :],
                            w_branch_attn[l].astype(BF16), w_branch_gmlp[l].astype(BF16),
                            w_out[l].astype(BF16), peer_w_query[l].astype(BF16), sk0, sk1, seq)

        e4, g4 = _route(st.reshape(2 * PEER_HEADS, PEER_NKEYS, t // LANES, LANES))
        e_t = e4.reshape(PEER_PICKS, t)
        g_t = g4.reshape(PEER_PICKS, t)
        uv = _pack_rows(peer_u[l], peer_v[l])
        x2 = _peer(e_t, h2, g_t, x1, gt2, post_g2[l][None, :], uv, seq)
    return x2.reshape(batch, seq, d)
```

```python
import functools
import math

import numpy as np
import jax
import jax.numpy as jnp
from jax import lax
from jax.experimental import pallas as pl
from jax.experimental.pallas import tpu as pltpu

F32 = jnp.float32
BF16 = jnp.bfloat16
I32 = jnp.int32

D_MODEL = 1024
ATTN_HEADS = 8
HEAD_DIM = 64
ATTN_WIDTH = ATTN_HEADS * HEAD_DIM
MOBA_BLOCK = 256
MOBA_TOPK = 3
REL_BUCKETS = 32
REL_MAX_DISTANCE = 2048
GMLP_WIDTH = D_MODEL // 2
GMLP_GROUPS = 8
GMLP_GROUP_DIM = GMLP_WIDTH // GMLP_GROUPS
GMLP_CHUNK = 128
PEER_HEADS = 8
PEER_NKEYS = 128
PEER_QDIM = 128
PEER_HALF = PEER_QDIM // 2
PEER_TOPK = 16
PEER_PICKS = PEER_HEADS * PEER_TOPK
IN_WIDTH = 3 * ATTN_WIDTH + 2 * GMLP_WIDTH + 2 * D_MODEL
NORM_EPS = 1e-6
NEG_INF = -1e30

LANES = 128
BIAS_TILES = 10
VMEM_LIMIT = 56 * 1024 * 1024


def _cparams(*sem):
    return pltpu.CompilerParams(dimension_semantics=sem, vmem_limit_bytes=VMEM_LIMIT)


def _rms(xf, g):
    return xf * lax.rsqrt(jnp.mean(xf * xf, axis=-1, keepdims=True) + NORM_EPS) * g


def _ada_kernel(c_ref, w_ref, b_ref, o_ref):
    c = c_ref[...]
    cond = c * jax.nn.sigmoid(c)
    o_ref[...] = jnp.dot(cond, w_ref[...], preferred_element_type=F32,
                         precision=lax.Precision.HIGHEST) + b_ref[...]


def _ada(c8, w, b):
    n = w.shape[1]
    tn = 768
    return pl.pallas_call(
        _ada_kernel,
        grid=(n // tn,),
        in_specs=[pl.BlockSpec((8, D_MODEL), lambda j: (0, 0)),
                  pl.BlockSpec((D_MODEL, tn), lambda j: (0, j)),
                  pl.BlockSpec((1, tn), lambda j: (0, j))],
        out_specs=pl.BlockSpec((8, tn), lambda j: (0, j)),
        out_shape=jax.ShapeDtypeStruct((8, n), F32),
        compiler_params=_cparams("arbitrary"),
    )(c8, w, b)


def _bucket_table():
    max_exact = REL_BUCKETS // 2
    d = np.arange(BIAS_TILES)[:, None, None]
    c = np.arange(MOBA_BLOCK)[None, :, None]
    r = np.arange(MOBA_BLOCK)[None, None, :]
    n = np.maximum(d * MOBA_BLOCK + r - c, 0)
    nf = np.maximum(n, max_exact).astype(np.float32)
    large = max_exact + (np.log(nf / np.float32(max_exact)) / np.float32(math.log(REL_MAX_DISTANCE / max_exact))
                         * np.float32(REL_BUCKETS - max_exact)).astype(np.int32)
    large = np.minimum(large, REL_BUCKETS - 1)
    return np.where(n < max_exact, n, large).astype(np.int32)


def _bias_kernel(rb_ref, bucket_ref, o_ref):
    h = pl.program_id(0)
    for d in range(BIAS_TILES):
        bk = bucket_ref[d]
        acc = jnp.zeros(bk.shape, F32)
        for b in range(REL_BUCKETS):
            acc = jnp.where(bk == b, rb_ref[b, h], acc)
        if d == 0:
            key = lax.broadcasted_iota(I32, bk.shape, 0)
            qry = lax.broadcasted_iota(I32, bk.shape, 1)
            acc = jnp.where(qry >= key, acc, NEG_INF)
        o_ref[0, d] = acc


def _bias_tiles(rel_bias):
    bucket = jnp.asarray(_bucket_table())
    blk = (BIAS_TILES, MOBA_BLOCK, MOBA_BLOCK)
    return pl.pallas_call(
        _bias_kernel,
        grid=(ATTN_HEADS,),
        in_specs=[pl.BlockSpec(memory_space=pltpu.SMEM),
                  pl.BlockSpec(blk, lambda h: (0, 0, 0))],
        out_specs=pl.BlockSpec((1,) + blk, lambda h: (h, 0, 0, 0)),
        out_shape=jax.ShapeDtypeStruct((ATTN_HEADS,) + blk, F32),
        compiler_params=_cparams("arbitrary"),
    )(rel_bias, bucket)


IN_TM = MOBA_BLOCK
IN_NCHUNK = 512


def _inproj_kernel(x_ref, sc_ref, sh_ref, g_ref, w_ref, wvt_ref,
                   q_ref, k_ref, vt_ref, zu_ref, zv_ref, ga_ref, gg_ref):
    x = x_ref[...]
    h = _rms(x, g_ref[...]) * (1.0 + sc_ref[0]) + sh_ref[0]
    hb = h.astype(BF16)
    nt = (((1,), (1,)), ((), ()))
    vt_ref[0, 0] = lax.dot_general(wvt_ref[...], hb, nt, preferred_element_type=F32).astype(vt_ref.dtype)
    outs = ((q_ref, ATTN_WIDTH), (k_ref, ATTN_WIDTH), (None, ATTN_WIDTH),
            (zu_ref, GMLP_WIDTH), (zv_ref, GMLP_WIDTH), (ga_ref, D_MODEL), (gg_ref, D_MODEL))
    off = 0
    for ref, width in outs:
        for c0 in range(0, width if ref is not None else 0, IN_NCHUNK):
            p = jnp.dot(hb, w_ref[:, off + c0:off + c0 + IN_NCHUNK], preferred_element_type=F32)
            ref[:, c0:c0 + IN_NCHUNK] = p.astype(ref.dtype)
        off += width


def _inproj(x2, sc1, sh1, pre_g, w_in_bf, wvt_bf, batch, seq):
    t = x2.shape[0]
    tm = IN_TM
    per_b = seq // tm
    row = lambda w: pl.BlockSpec((tm, w), lambda i: (i, 0))
    mod = pl.BlockSpec((1, 1, D_MODEL), lambda i: (i // per_b, 0, 0))
    return pl.pallas_call(
        _inproj_kernel,
        grid=(t // tm,),
        in_specs=[row(D_MODEL), mod, mod,
                  pl.BlockSpec((1, D_MODEL), lambda i: (0, 0)),
                  pl.BlockSpec((D_MODEL, IN_WIDTH), lambda i: (0, 0)),
                  pl.BlockSpec((ATTN_WIDTH, D_MODEL), lambda i: (0, 0))],
        out_specs=[row(ATTN_WIDTH), row(ATTN_WIDTH),
                   pl.BlockSpec((1, 1, ATTN_WIDTH, tm), lambda i: (i // per_b, i % per_b, 0, 0)),
                   row(GMLP_WIDTH), row(GMLP_WIDTH), row(D_MODEL), row(D_MODEL)],
        out_shape=[jax.ShapeDtypeStruct((t, ATTN_WIDTH), BF16)] * 2
        + [jax.ShapeDtypeStruct((batch, per_b, ATTN_WIDTH, tm), BF16)]
        + [jax.ShapeDtypeStruct((t, GMLP_WIDTH), F32)] * 2
        + [jax.ShapeDtypeStruct((t, D_MODEL), F32)] * 2,
        compiler_params=_cparams("arbitrary"),
    )(x2, sc1, sh1, pre_g, w_in_bf, wvt_bf)


GMLP_TM = 512


def _gmlp_kernel(zu_ref, zv_ref, lg_ref, lb_ref, ws_ref, bs_ref, y_ref):
    u = jax.nn.gelu(zu_ref[...])
    gv = jax.nn.gelu(zv_ref[...])
    mu = jnp.mean(gv, axis=-1, keepdims=True)
    vc = gv - mu
    var = jnp.mean(vc * vc, axis=-1, keepdims=True)
    v = (vc * lax.rsqrt(var + NORM_EPS) * lg_ref[...] + lb_ref[...]).astype(BF16)
    tri = (lax.broadcasted_iota(I32, (GMLP_CHUNK, GMLP_CHUNK), 0)
           >= lax.broadcasted_iota(I32, (GMLP_CHUNK, GMLP_CHUNK), 1))
    lane = lax.broadcasted_iota(I32, (GMLP_CHUNK, LANES), 1)
    lo = lane < GMLP_GROUP_DIM
    zero = jnp.zeros((), BF16)
    for pair in range(GMLP_GROUPS // 2):
        w0 = jnp.where(tri, ws_ref[2 * pair], 0.0).astype(BF16)
        w1 = jnp.where(tri, ws_ref[2 * pair + 1], 0.0).astype(BF16)
        b0 = bs_ref[:, 2 * pair:2 * pair + 1]
        b1 = bs_ref[:, 2 * pair + 1:2 * pair + 2]
        bias = jnp.where(lo, b0, b1)
        cols = slice(pair * LANES, (pair + 1) * LANES)
        for ch in range(GMLP_TM // GMLP_CHUNK):
            rows = slice(ch * GMLP_CHUNK, (ch + 1) * GMLP_CHUNK)
            v2 = v[rows, cols]
            mixed = (jnp.dot(w0, jnp.where(lo, v2, zero), preferred_element_type=F32)
                     + jnp.dot(w1, jnp.where(lo, zero, v2), preferred_element_type=F32) + bias)
            y_ref[rows, cols] = (u[rows, cols] * mixed).astype(y_ref.dtype)


def _gmlp(zu, zv, ln_g, ln_b, w_s, bs_t):
    t = zu.shape[0]
    tm = GMLP_TM
    row = pl.BlockSpec((tm, GMLP_WIDTH), lambda i: (i, 0))
    vec = pl.BlockSpec((1, GMLP_WIDTH), lambda i: (0, 0))
    return pl.pallas_call(
        _gmlp_kernel,
        grid=(t // tm,),
        in_specs=[row, row, vec, vec,
                  pl.BlockSpec((GMLP_GROUPS, GMLP_CHUNK, GMLP_CHUNK), lambda i: (0, 0, 0)),
                  pl.BlockSpec((GMLP_CHUNK, GMLP_GROUPS), lambda i: (0, 0))],
        out_specs=row,
        out_shape=jax.ShapeDtypeStruct((t, GMLP_WIDTH), BF16),
        compiler_params=_cparams("arbitrary"),
    )(zu, zv, ln_g, ln_b, w_s, bs_t)


MOBA_GROUP = 4


def _moba_kernel(q_ref, k_ref, vt_ref, bias_ref, o_ref, kmean_sc, *, nb, n_slots):
    i = pl.program_id(2)
    blk = MOBA_BLOCK
    scale = HEAD_DIM ** -0.5
    nt = (((1,), (1,)), ((), ()))

    @pl.when(i == 0)
    def _():
        def body(j, c):
            kb = k_ref[pl.ds(pl.multiple_of(j * blk, blk), blk), :].astype(F32)
            kmean_sc[pl.ds(j, 1), :] = jnp.mean(kb, axis=0, keepdims=True)
            return c
        lax.fori_loop(0, nb, body, 0)

    q2 = q_ref[...]
    km = kmean_sc[...].astype(BF16)
    lane_q = lax.broadcasted_iota(I32, (blk, LANES), 1)
    blk_id = lax.broadcasted_iota(I32, (nb, blk), 0).astype(F32)
    i_f = i.astype(F32)
    zero = jnp.zeros((), BF16)

    qs, sels = [], []
    for hh in range(2):
        head_lanes = (lane_q < HEAD_DIM) if hh == 0 else (lane_q >= HEAD_DIM)
        qh = jnp.where(head_lanes, q2, zero)
        qs.append((qh.astype(F32) * scale).astype(BF16))
        score = lax.dot_general(km, qh, nt, preferred_element_type=F32)
        score = jnp.where(blk_id < i_f, score, NEG_INF)
        sel = []
        for r in range(n_slots):
            m = jnp.max(score, axis=0, keepdims=True)
            idx = jnp.min(jnp.where(score == m, blk_id, float(nb)), axis=0, keepdims=True)
            sel.append(jnp.where(i > r, idx, -1.0))
            score = jnp.where(blk_id == idx, -jnp.inf, score)
        sels.append(sel)

    def tile(jb, hh):
        jb_f = jb.astype(F32)
        kb = k_ref[pl.ds(pl.multiple_of(jb * blk, blk), blk), :]
        dist = jnp.clip(i - jb, 0, BIAS_TILES - 1)
        hit = jb_f == i_f
        for sr in sels[hh]:
            hit = hit | (sr == jb_f)
        s = lax.dot_general(kb, qs[hh], nt, preferred_element_type=F32) + bias_ref[hh, dist]
        return s, hit, jnp.where(hit, jnp.max(s, axis=0, keepdims=True), NEG_INF)

    grp = MOBA_GROUP

    def body(step, carry):
        first = (i // grp - step) * grp
        new = []
        for hh in range(2):
            m_run, l_run, acc = carry[hh]
            tiles = [tile(first + g, hh) for g in range(grp)]
            m_new = functools.reduce(jnp.maximum, [m_run] + [tl[2] for tl in tiles])
            l_new = jnp.exp(m_run - m_new) * l_run
            acc_new = jnp.exp(m_run - m_new) * acc
            for g, (s, hit, _) in enumerate(tiles):
                p = jnp.exp(s - jnp.where(hit, m_new, m_new - NEG_INF))
                l_new = l_new + jnp.sum(p, axis=0, keepdims=True)
                acc_new = acc_new + jnp.dot(vt_ref[0, first + g], p.astype(BF16), preferred_element_type=F32)
            new.append((m_new, l_new, acc_new))
        return tuple(new)

    init = tuple((jnp.full((1, blk), NEG_INF, F32), jnp.zeros((1, blk), F32), jnp.zeros((LANES, blk), F32))
                 for _ in range(2))
    fin = lax.fori_loop(0, i // grp + 1, body, init)
    o0 = fin[0][2] / fin[0][1]
    o1 = fin[1][2] / fin[1][1]
    out_t = jnp.concatenate([o0[:HEAD_DIM], o1[HEAD_DIM:]], axis=0)
    o_ref[...] = out_t.T.astype(o_ref.dtype)


def _moba(q, k, vt, bias, batch, seq):
    t = q.shape[0]
    nb = seq // MOBA_BLOCK
    n_slots = min(MOBA_TOPK, nb - 1)
    pairs = ATTN_HEADS // 2
    qo = pl.BlockSpec((MOBA_BLOCK, LANES), lambda hp, b, i: (b * nb + i, hp))
    return pl.pallas_call(
        functools.partial(_moba_kernel, nb=nb, n_slots=n_slots),
        grid=(pairs, batch, nb),
        in_specs=[qo,
                  pl.BlockSpec((seq, LANES), lambda hp, b, i: (b, hp)),
                  pl.BlockSpec((1, nb, LANES, MOBA_BLOCK), lambda hp, b, i: (b, 0, hp, 0)),
                  pl.BlockSpec((2, BIAS_TILES, MOBA_BLOCK, MOBA_BLOCK), lambda hp, b, i: (hp, 0, 0, 0))],
        out_specs=qo,
        out_shape=jax.ShapeDtypeStruct((t, ATTN_WIDTH), BF16),
        scratch_shapes=[pltpu.VMEM((nb, LANES), F32)],
        compiler_params=_cparams("arbitrary", "arbitrary", "arbitrary"),
    )(q, k, vt, bias)


MERGE_TM = 256


def _merge_kernel(x_ref, ya_ref, yg_ref, ga_ref, gg_ref, gt1_ref, sc2_ref, sh2_ref,
                  pg1_ref, g2_ref, wa_ref, wg_ref, wo_ref, wq_ref, sk0_ref, sk1_ref,
                  x1_ref, h2_ref, st_ref):
    nt = (((1,), (1,)), ((), ()))
    la = jnp.dot(ya_ref[...], wa_ref[...], preferred_element_type=F32)
    lg = jnp.dot(yg_ref[...], wg_ref[...], preferred_element_type=F32)
    merged = jax.nn.sigmoid(ga_ref[...]) * la + jax.nn.sigmoid(gg_ref[...]) * lg
    y = jnp.dot(merged.astype(BF16), wo_ref[...], preferred_element_type=F32)
    x1 = x_ref[...] + gt1_ref[0] * _rms(y, pg1_ref[...])
    x1_ref[...] = x1
    h2 = _rms(x1, g2_ref[...]) * (1.0 + sc2_ref[0]) + sh2_ref[0]
    h2_ref[...] = h2
    qp = jnp.dot(h2.astype(BF16), wq_ref[...], preferred_element_type=F32).astype(BF16)
    for h in range(PEER_HEADS):
        qh = qp[:, h * PEER_QDIM:(h + 1) * PEER_QDIM]
        st_ref[2 * h] = lax.dot_general(sk0_ref[...], qh, nt, preferred_element_type=F32)
        st_ref[2 * h + 1] = lax.dot_general(sk1_ref[...], qh, nt, preferred_element_type=F32)


def _merge(x2, ya, yg, ga, gg, gt1, sc2, sh2, post_g1, pre_g2, wa, wg, wo, wq, sk0, sk1, seq):
    t = x2.shape[0]
    tm = MERGE_TM
    per_b = seq // tm
    row = lambda w: pl.BlockSpec((tm, w), lambda i: (i, 0))
    mod = pl.BlockSpec((1, 1, D_MODEL), lambda i: (i // per_b, 0, 0))
    vec = pl.BlockSpec((1, D_MODEL), lambda i: (0, 0))
    full = lambda a: pl.BlockSpec(a.shape, lambda i: (0,) * a.ndim)
    return pl.pallas_call(
        _merge_kernel,
        grid=(t // tm,),
        in_specs=[row(D_MODEL), row(ATTN_WIDTH), row(GMLP_WIDTH), row(D_MODEL), row(D_MODEL),
                  mod, mod, mod, vec, vec, full(wa), full(wg), full(wo), full(wq), full(sk0), full(sk1)],
        out_specs=[row(D_MODEL), row(D_MODEL),
                   pl.BlockSpec((2 * PEER_HEADS, PEER_NKEYS, tm), lambda i: (0, 0, i))],
        out_shape=[jax.ShapeDtypeStruct((t, D_MODEL), F32), jax.ShapeDtypeStruct((t, D_MODEL), F32),
                   jax.ShapeDtypeStruct((2 * PEER_HEADS, PEER_NKEYS, t), F32)],
        compiler_params=_cparams("arbitrary"),
    )(x2, ya, yg, ga, gg, gt1, sc2, sh2, post_g1, pre_g2, wa, wg, wo, wq, sk0, sk1)


ROUTE_SUB = 8
_CANDS = [(a, b) for a in range(PEER_TOPK) for b in range(PEER_TOPK) if (a + 1) * (b + 1) <= PEER_TOPK]


def _route_kernel(s_ref, e_ref, g_ref, buf, buf2):
    kio = lax.broadcasted_iota(I32, (PEER_NKEYS, ROUTE_SUB, LANES), 0).astype(F32)

    def fold(pair_op, reduce_op, a):
        q = PEER_NKEYS // 4
        return reduce_op(pair_op(pair_op(a[0:q], a[q:2 * q]), pair_op(a[2 * q:3 * q], a[3 * q:4 * q])), axis=0)

    def step(buf):
        s = buf[...]
        m = fold(jnp.maximum, jnp.max, s)
        idx = fold(jnp.minimum, jnp.min, jnp.where(s == m[None], kio, float(PEER_NKEYS)))
        buf[...] = jnp.where(kio == idx[None], -jnp.inf, s)
        return m, idx

    buf[...] = s_ref[0]
    buf2[...] = s_ref[1]
    v1, i1, v2, i2 = [], [], [], []
    for _ in range(PEER_TOPK):
        m, idx = step(buf)
        v1.append(m)
        i1.append(idx)
        m, idx = step(buf2)
        v2.append(m)
        i2.append(idx)
    cand = [v1[a] + v2[b] for a, b in _CANDS]
    eid = [i1[a] * float(PEER_NKEYS) + i2[b] for a, b in _CANDS]
    flat = [float(a * PEER_TOPK + b) for a, b in _CANDS]
    sc, es = [], []
    for _ in range(PEER_TOPK):
        m = functools.reduce(jnp.maximum, cand)
        idx = functools.reduce(jnp.minimum,
                               [jnp.where(c == m, f, float(PEER_TOPK * PEER_TOPK)) for c, f in zip(cand, flat)])
        es.append(functools.reduce(jnp.maximum, [jnp.where(idx == f, e, -1.0) for e, f in zip(eid, flat)]))
        cand = [jnp.where(idx == f, -jnp.inf, c) for c, f in zip(cand, flat)]
        sc.append(m)
    ex = [jnp.exp(s - sc[0]) for s in sc]
    tot = functools.reduce(jnp.add, ex)
    for r in range(PEER_TOPK):
        e_ref[0, r] = es[r].astype(I32)
        g_ref[0, r] = ex[r] / tot


def _route(st4):
    ntile = st4.shape[2]
    grp = ROUTE_SUB
    out = pl.BlockSpec((1, PEER_TOPK, grp, LANES), lambda h, g: (h, 0, g, 0))
    shp = (PEER_HEADS, PEER_TOPK, ntile, LANES)
    return pl.pallas_call(
        _route_kernel,
        grid=(PEER_HEADS, ntile // grp),
        in_specs=[pl.BlockSpec((2, PEER_NKEYS, grp, LANES), lambda h, g: (h, 0, g, 0))],
        out_specs=[out, out],
        out_shape=[jax.ShapeDtypeStruct(shp, I32), jax.ShapeDtypeStruct(shp, F32)],
        scratch_shapes=[pltpu.VMEM((PEER_NKEYS, grp, LANES), F32)] * 2,
        compiler_params=_cparams("arbitrary", "arbitrary"),
    )(st4)


PEER_TT = 128
PEER_NBUF = 8
SUBLANES = 8
PEER_KG = PEER_PICKS // SUBLANES
PEER_RT = D_MODEL // LANES


def _pack_rows(u, v):
    ub = lax.bitcast_convert_type(u.astype(BF16), jnp.uint16).astype(jnp.uint32)
    vb = lax.bitcast_convert_type(v.astype(BF16), jnp.uint16).astype(jnp.uint32)
    return (ub | (vb << 16)).reshape(u.shape[0], PEER_RT, LANES)


def _peer_kernel(e_ref, h2_ref, gt_ref, x1_ref, gt2_ref, pg2_ref, uv_ref, o_ref, *scratch):
    bufs = scratch[:PEER_NBUF]
    sem, y_sc = scratch[PEER_NBUF:]
    tt = PEER_TT
    ahead = PEER_NBUF - 1

    def issue(t, slot):
        for k in range(PEER_PICKS):
            pltpu.make_async_copy(uv_ref.at[e_ref[t, k]],
                                  bufs[slot].at[k // SUBLANES, :, k % SUBLANES], sem.at[slot]
                                  ).start(priority=k % 2)

    def wait(slot):
        pltpu.make_async_copy(bufs[slot], bufs[slot], sem.at[slot]).wait()

    lane_t = lax.broadcasted_iota(I32, (PEER_PICKS, tt), 1)
    himask = jnp.uint32(0xFFFF0000)

    def compute(t, slot):
        buf = bufs[slot]
        xb = h2_ref[pl.ds(t, 1), :]
        xr = [xb[:, r * LANES:(r + 1) * LANES] for r in range(PEER_RT)]
        parts = []
        for kg in range(PEER_KG):
            acc = None
            for r in range(PEER_RT):
                term = lax.bitcast_convert_type(buf[kg, r] << 16, F32) * xr[r]
                acc = term if acc is None else acc + term
            parts.append(acc)
        a = jnp.sum(jnp.concatenate(parts, axis=0), axis=-1, keepdims=True)
        gcol = jnp.sum(jnp.where(lane_t == t, gt_ref[...], 0.0), axis=-1, keepdims=True)
        wb = jnp.broadcast_to(gcol * jax.nn.gelu(a), (PEER_PICKS, LANES))
        cols = []
        for r in range(PEER_RT):
            acc = None
            for kg in range(PEER_KG):
                term = (wb[kg * SUBLANES:(kg + 1) * SUBLANES]
                        * lax.bitcast_convert_type(buf[kg, r] & himask, F32))
                acc = term if acc is None else acc + term
            cols.append(jnp.sum(acc, axis=0, keepdims=True))
        y_sc[pl.ds(t, 1), :] = jnp.concatenate(cols, axis=-1)

    for t in range(ahead):
        issue(t, t)

    def group(g, c):
        for s in range(PEER_NBUF):
            t = g * PEER_NBUF + s
            issue(t + ahead, (s + ahead) % PEER_NBUF)
            wait(s)
            compute(t, s)
        return c

    ngroups = tt // PEER_NBUF
    lax.fori_loop(0, ngroups - 1, group, 0)
    for s in range(PEER_NBUF):
        t = (ngroups - 1) * PEER_NBUF + s
        if s == 0:
            issue(t + ahead, ahead)
        wait(s)
        compute(t, s)
    o_ref[...] = x1_ref[...] + gt2_ref[0] * _rms(y_sc[...], pg2_ref[...])


def _peer(e_t, h2, g_t, x1, gt2, post_g2, uv, seq):
    t = h2.shape[0]
    tt = PEER_TT
    per_b = seq // tt
    row = pl.BlockSpec((tt, D_MODEL), lambda i: (i, 0))
    return pl.pallas_call(
        _peer_kernel,
        grid=(t // tt,),
        in_specs=[pl.BlockSpec((tt, PEER_PICKS), lambda i: (i, 0), memory_space=pltpu.SMEM),
                  row,
                  pl.BlockSpec((PEER_PICKS, tt), lambda i: (0, i)),
                  row,
                  pl.BlockSpec((1, 1, D_MODEL), lambda i: (i // per_b, 0, 0)),
                  pl.BlockSpec((1, D_MODEL), lambda i: (0, 0)),
                  pl.BlockSpec(memory_space=pl.ANY)],
        out_specs=row,
        out_shape=jax.ShapeDtypeStruct((t, D_MODEL), F32),
        scratch_shapes=[pltpu.VMEM((PEER_KG, PEER_RT, SUBLANES, LANES), jnp.uint32) for _ in range(PEER_NBUF)]
        + [pltpu.SemaphoreType.DMA((PEER_NBUF,)), pltpu.VMEM((tt, D_MODEL), F32)],
        compiler_params=_cparams("arbitrary"),
    )(e_t, h2, g_t, x1, gt2, post_g2, uv)


def kernel(x, c, ada_w, ada_b, pre_g1, post_g1, w_in, gmlp_ln_g, gmlp_ln_b, gmlp_w_s, gmlp_b_s,
           rel_bias, w_branch_attn, w_branch_gmlp, w_out, pre_g2, post_g2, peer_w_query,
           peer_sub_keys, peer_u, peer_v):
    batch, seq, d = x.shape
    depth = ada_w.shape[0]
    t = batch * seq
    assert d == D_MODEL and batch <= 8 and seq % (MOBA_GROUP * MOBA_BLOCK) == 0 and t % (ROUTE_SUB * LANES) == 0

    c8 = jnp.zeros((8, d), F32).at[:batch].set(c)
    bias = _bias_tiles(rel_bias)
    x2 = x.reshape(t, d)
    for l in range(depth):
        mod = _ada(c8, ada_w[l], ada_b[l][None, :])[:batch]
        sh1, sc1, gt1, sh2, sc2, gt2 = [m[:, None, :] for m in jnp.split(mod, 6, axis=-1)]

        w_in_bf = w_in[l].astype(BF16)
        wvt = w_in_bf[:, 2 * ATTN_WIDTH:3 * ATTN_WIDTH].T
        q, k, v, zu, zv, ga, gg = _inproj(x2, sc1, sh1, pre_g1[l][None, :], w_in_bf, wvt, batch, seq)
        yg = _gmlp(zu, zv, gmlp_ln_g[l][None, :], gmlp_ln_b[l][None, :], gmlp_w_s[l], gmlp_b_s[l].T)
        ya = _moba(q, k, v, bias, batch, seq)

        zpad = jnp.zeros((PEER_NKEYS, PEER_HALF), F32)
        sk0 = jnp.concatenate([peer_sub_keys[l, 0], zpad], axis=1).astype(BF16)
        sk1 = jnp.concatenate([zpad, peer_sub_keys[l, 1]], axis=1).astype(BF16)
        x1, h2, st = _merge(x2, ya, yg, ga, gg, gt1, sc2, sh2, post_g1[l][None, :], pre_g2[l][None, :],
                            w_branch_attn[l].astype(BF16), w_branch_gmlp[l].astype(BF16),
                            w_out[l].astype(BF16), peer_w_query[l].astype(BF16), sk0, sk1, seq)

        e4, g4 = _route(st.reshape(2 * PEER_HEADS, PEER_NKEYS, t // LANES, LANES))
        e_t = e4.reshape(PEER_PICKS, t).T
        g_t = g4.reshape(PEER_PICKS, t)
        uv = _pack_rows(peer_u[l], peer_v[l])
        x2 = _peer(e_t, h2, g_t, x1, gt2, post_g2[l][None, :], uv, seq)
    return x2.reshape(batch, seq, d)
```

```python
import functools
import math

import numpy as np
import jax
import jax.numpy as jnp
from jax import lax
from jax.experimental import pallas as pl
from jax.experimental.pallas import tpu as pltpu

F32 = jnp.float32
BF16 = jnp.bfloat16
I32 = jnp.int32

D_MODEL = 1024
ATTN_HEADS = 8
HEAD_DIM = 64
ATTN_WIDTH = ATTN_HEADS * HEAD_DIM
MOBA_BLOCK = 256
MOBA_TOPK = 3
REL_BUCKETS = 32
REL_MAX_DISTANCE = 2048
GMLP_WIDTH = D_MODEL // 2
GMLP_GROUPS = 8
GMLP_GROUP_DIM = GMLP_WIDTH // GMLP_GROUPS
GMLP_CHUNK = 128
PEER_HEADS = 8
PEER_NKEYS = 128
PEER_QDIM = 128
PEER_HALF = PEER_QDIM // 2
PEER_TOPK = 16
PEER_PICKS = PEER_HEADS * PEER_TOPK
IN_WIDTH = 3 * ATTN_WIDTH + 2 * GMLP_WIDTH + 2 * D_MODEL
NORM_EPS = 1e-6
NEG_INF = -1e30

LANES = 128
BIAS_TILES = 10
VMEM_LIMIT = 56 * 1024 * 1024


def _cparams(*sem):
    return pltpu.CompilerParams(dimension_semantics=sem, vmem_limit_bytes=VMEM_LIMIT)


def _rms(xf, g):
    return xf * lax.rsqrt(jnp.mean(xf * xf, axis=-1, keepdims=True) + NORM_EPS) * g


def _ada_kernel(c_ref, w_ref, b_ref, o_ref):
    c = c_ref[...]
    cond = c * jax.nn.sigmoid(c)
    o_ref[...] = jnp.dot(cond, w_ref[...], preferred_element_type=F32,
                         precision=lax.Precision.HIGHEST) + b_ref[...]


def _ada(c8, w, b):
    n = w.shape[1]
    tn = 768
    return pl.pallas_call(
        _ada_kernel,
        grid=(n // tn,),
        in_specs=[pl.BlockSpec((8, D_MODEL), lambda j: (0, 0)),
                  pl.BlockSpec((D_MODEL, tn), lambda j: (0, j)),
                  pl.BlockSpec((1, tn), lambda j: (0, j))],
        out_specs=pl.BlockSpec((8, tn), lambda j: (0, j)),
        out_shape=jax.ShapeDtypeStruct((8, n), F32),
        compiler_params=_cparams("arbitrary"),
    )(c8, w, b)


def _bucket_table():
    max_exact = REL_BUCKETS // 2
    d = np.arange(BIAS_TILES)[:, None, None]
    c = np.arange(MOBA_BLOCK)[None, :, None]
    r = np.arange(MOBA_BLOCK)[None, None, :]
    n = np.maximum(d * MOBA_BLOCK + r - c, 0)
    nf = np.maximum(n, max_exact).astype(np.float32)
    large = max_exact + (np.log(nf / np.float32(max_exact)) / np.float32(math.log(REL_MAX_DISTANCE / max_exact))
                         * np.float32(REL_BUCKETS - max_exact)).astype(np.int32)
    large = np.minimum(large, REL_BUCKETS - 1)
    return np.where(n < max_exact, n, large).astype(np.int32)


def _bias_kernel(rb_ref, bucket_ref, o_ref):
    h = pl.program_id(0)
    for d in range(BIAS_TILES):
        bk = bucket_ref[d]
        acc = jnp.zeros(bk.shape, F32)
        for b in range(REL_BUCKETS):
            acc = jnp.where(bk == b, rb_ref[b, h], acc)
        if d == 0:
            key = lax.broadcasted_iota(I32, bk.shape, 0)
            qry = lax.broadcasted_iota(I32, bk.shape, 1)
            acc = jnp.where(qry >= key, acc, NEG_INF)
        o_ref[0, d] = acc


def _bias_tiles(rel_bias):
    bucket = jnp.asarray(_bucket_table())
    blk = (BIAS_TILES, MOBA_BLOCK, MOBA_BLOCK)
    return pl.pallas_call(
        _bias_kernel,
        grid=(ATTN_HEADS,),
        in_specs=[pl.BlockSpec(memory_space=pltpu.SMEM),
                  pl.BlockSpec(blk, lambda h: (0, 0, 0))],
        out_specs=pl.BlockSpec((1,) + blk, lambda h: (h, 0, 0, 0)),
        out_shape=jax.ShapeDtypeStruct((ATTN_HEADS,) + blk, F32),
        compiler_params=_cparams("arbitrary"),
    )(rel_bias, bucket)


IN_TM = MOBA_BLOCK
IN_NCHUNK = 512


def _inproj_kernel(x_ref, sc_ref, sh_ref, g_ref, w_ref, wvt_ref,
                   q_ref, k_ref, vt_ref, zu_ref, zv_ref, ga_ref, gg_ref):
    x = x_ref[...]
    h = _rms(x, g_ref[...]) * (1.0 + sc_ref[0]) + sh_ref[0]
    hb = h.astype(BF16)
    nt = (((1,), (1,)), ((), ()))
    vt_ref[0, 0] = lax.dot_general(wvt_ref[...], hb, nt, preferred_element_type=F32).astype(vt_ref.dtype)
    outs = ((q_ref, ATTN_WIDTH), (k_ref, ATTN_WIDTH), (None, ATTN_WIDTH),
            (zu_ref, GMLP_WIDTH), (zv_ref, GMLP_WIDTH), (ga_ref, D_MODEL), (gg_ref, D_MODEL))
    off = 0
    for ref, width in outs:
        for c0 in range(0, width if ref is not None else 0, IN_NCHUNK):
            p = jnp.dot(hb, w_ref[:, off + c0:off + c0 + IN_NCHUNK], preferred_element_type=F32)
            ref[:, c0:c0 + IN_NCHUNK] = p.astype(ref.dtype)
        off += width


def _inproj(x2, sc1, sh1, pre_g, w_in_bf, wvt_bf, batch, seq):
    t = x2.shape[0]
    tm = IN_TM
    per_b = seq // tm
    row = lambda w: pl.BlockSpec((tm, w), lambda i: (i, 0))
    mod = pl.BlockSpec((1, 1, D_MODEL), lambda i: (i // per_b, 0, 0))
    return pl.pallas_call(
        _inproj_kernel,
        grid=(t // tm,),
        in_specs=[row(D_MODEL), mod, mod,
                  pl.BlockSpec((1, D_MODEL), lambda i: (0, 0)),
                  pl.BlockSpec((D_MODEL, IN_WIDTH), lambda i: (0, 0)),
                  pl.BlockSpec((ATTN_WIDTH, D_MODEL), lambda i: (0, 0))],
        out_specs=[row(ATTN_WIDTH), row(ATTN_WIDTH),
                   pl.BlockSpec((1, 1, ATTN_WIDTH, tm), lambda i: (i // per_b, i % per_b, 0, 0)),
                   row(GMLP_WIDTH), row(GMLP_WIDTH), row(D_MODEL), row(D_MODEL)],
        out_shape=[jax.ShapeDtypeStruct((t, ATTN_WIDTH), BF16)] * 2
        + [jax.ShapeDtypeStruct((batch, per_b, ATTN_WIDTH, tm), BF16)]
        + [jax.ShapeDtypeStruct((t, GMLP_WIDTH), F32)] * 2
        + [jax.ShapeDtypeStruct((t, D_MODEL), F32)] * 2,
        compiler_params=_cparams("arbitrary"),
    )(x2, sc1, sh1, pre_g, w_in_bf, wvt_bf)


GMLP_TM = 512


def _gmlp_kernel(zu_ref, zv_ref, lg_ref, lb_ref, ws_ref, bs_ref, y_ref):
    u = jax.nn.gelu(zu_ref[...])
    gv = jax.nn.gelu(zv_ref[...])
    mu = jnp.mean(gv, axis=-1, keepdims=True)
    vc = gv - mu
    var = jnp.mean(vc * vc, axis=-1, keepdims=True)
    v = (vc * lax.rsqrt(var + NORM_EPS) * lg_ref[...] + lb_ref[...]).astype(BF16)
    tri = (lax.broadcasted_iota(I32, (GMLP_CHUNK, GMLP_CHUNK), 0)
           >= lax.broadcasted_iota(I32, (GMLP_CHUNK, GMLP_CHUNK), 1))
    lane = lax.broadcasted_iota(I32, (GMLP_CHUNK, LANES), 1)
    lo = lane < GMLP_GROUP_DIM
    zero = jnp.zeros((), BF16)
    for pair in range(GMLP_GROUPS // 2):
        w0 = jnp.where(tri, ws_ref[2 * pair], 0.0).astype(BF16)
        w1 = jnp.where(tri, ws_ref[2 * pair + 1], 0.0).astype(BF16)
        b0 = bs_ref[:, 2 * pair:2 * pair + 1]
        b1 = bs_ref[:, 2 * pair + 1:2 * pair + 2]
        bias = jnp.where(lo, b0, b1)
        cols = slice(pair * LANES, (pair + 1) * LANES)
        for ch in range(GMLP_TM // GMLP_CHUNK):
            rows = slice(ch * GMLP_CHUNK, (ch + 1) * GMLP_CHUNK)
            v2 = v[rows, cols]
            mixed = (jnp.dot(w0, jnp.where(lo, v2, zero), preferred_element_type=F32)
                     + jnp.dot(w1, jnp.where(lo, zero, v2), preferred_element_type=F32) + bias)
            y_ref[rows, cols] = (u[rows, cols] * mixed).astype(y_ref.dtype)


def _gmlp(zu, zv, ln_g, ln_b, w_s, bs_t):
    t = zu.shape[0]
    tm = GMLP_TM
    row = pl.BlockSpec((tm, GMLP_WIDTH), lambda i: (i, 0))
    vec = pl.BlockSpec((1, GMLP_WIDTH), lambda i: (0, 0))
    return pl.pallas_call(
        _gmlp_kernel,
        grid=(t // tm,),
        in_specs=[row, row, vec, vec,
                  pl.BlockSpec((GMLP_GROUPS, GMLP_CHUNK, GMLP_CHUNK), lambda i: (0, 0, 0)),
                  pl.BlockSpec((GMLP_CHUNK, GMLP_GROUPS), lambda i: (0, 0))],
        out_specs=row,
        out_shape=jax.ShapeDtypeStruct((t, GMLP_WIDTH), BF16),
        compiler_params=_cparams("arbitrary"),
    )(zu, zv, ln_g, ln_b, w_s, bs_t)


MOBA_GROUP = 4


def _moba_kernel(q_ref, k_ref, vt_ref, bias_ref, o_ref, kmean_sc, *, nb, n_slots):
    i = pl.program_id(2)
    blk = MOBA_BLOCK
    scale = HEAD_DIM ** -0.5
    nt = (((1,), (1,)), ((), ()))

    @pl.when(i == 0)
    def _():
        def body(j, c):
            kb = k_ref[pl.ds(pl.multiple_of(j * blk, blk), blk), :].astype(F32)
            kmean_sc[pl.ds(j, 1), :] = jnp.mean(kb, axis=0, keepdims=True)
            return c
        lax.fori_loop(0, nb, body, 0)

    q2 = q_ref[...]
    km = kmean_sc[...].astype(BF16)
    lane_q = lax.broadcasted_iota(I32, (blk, LANES), 1)
    blk_id = lax.broadcasted_iota(I32, (nb, blk), 0).astype(F32)
    i_f = i.astype(F32)
    zero = jnp.zeros((), BF16)

    qs, sels = [], []
    for hh in range(2):
        head_lanes = (lane_q < HEAD_DIM) if hh == 0 else (lane_q >= HEAD_DIM)
        qh = jnp.where(head_lanes, q2, zero)
        qs.append((qh.astype(F32) * scale).astype(BF16))
        score = lax.dot_general(km, qh, nt, preferred_element_type=F32)
        score = jnp.where(blk_id < i_f, score, NEG_INF)
        sel = []
        for r in range(n_slots):
            m = jnp.max(score, axis=0, keepdims=True)
            idx = jnp.min(jnp.where(score == m, blk_id, float(nb)), axis=0, keepdims=True)
            sel.append(jnp.where(i > r, idx, -1.0))
            score = jnp.where(blk_id == idx, -jnp.inf, score)
        sels.append(sel)

    def tile(jb, hh):
        jb_f = jb.astype(F32)
        kb = k_ref[pl.ds(pl.multiple_of(jb * blk, blk), blk), :]
        dist = jnp.clip(i - jb, 0, BIAS_TILES - 1)
        hit = jb_f == i_f
        for sr in sels[hh]:
            hit = hit | (sr == jb_f)
        s = lax.dot_general(kb, qs[hh], nt, preferred_element_type=F32) + bias_ref[hh, dist]
        return s, hit, jnp.where(hit, jnp.max(s, axis=0, keepdims=True), NEG_INF)

    grp = MOBA_GROUP

    def body(step, carry):
        first = (i // grp - step) * grp
        new = []
        for hh in range(2):
            m_run, l_run, acc = carry[hh]
            tiles = [tile(first + g, hh) for g in range(grp)]
            m_new = functools.reduce(jnp.maximum, [m_run] + [tl[2] for tl in tiles])
            l_new = jnp.exp(m_run - m_new) * l_run
            acc_new = jnp.exp(m_run - m_new) * acc
            for g, (s, hit, _) in enumerate(tiles):
                p = jnp.exp(s - jnp.where(hit, m_new, m_new - NEG_INF))
                l_new = l_new + jnp.sum(p, axis=0, keepdims=True)
                acc_new = acc_new + jnp.dot(vt_ref[0, first + g], p.astype(BF16), preferred_element_type=F32)
            new.append((m_new, l_new, acc_new))
        return tuple(new)

    init = tuple((jnp.full((1, blk), NEG_INF, F32), jnp.zeros((1, blk), F32), jnp.zeros((LANES, blk), F32))
                 for _ in range(2))
    fin = lax.fori_loop(0, i // grp + 1, body, init)
    o0 = fin[0][2] / fin[0][1]
    o1 = fin[1][2] / fin[1][1]
    out_t = jnp.concatenate([o0[:HEAD_DIM], o1[HEAD_DIM:]], axis=0)
    o_ref[...] = out_t.T.astype(o_ref.dtype)


def _moba(q, k, vt, bias, batch, seq):
    t = q.shape[0]
    nb = seq // MOBA_BLOCK
    n_slots = min(MOBA_TOPK, nb - 1)
    pairs = ATTN_HEADS // 2
    qo = pl.BlockSpec((MOBA_BLOCK, LANES), lambda hp, b, i: (b * nb + i, hp))
    return pl.pallas_call(
        functools.partial(_moba_kernel, nb=nb, n_slots=n_slots),
        grid=(pairs, batch, nb),
        in_specs=[qo,
                  pl.BlockSpec((seq, LANES), lambda hp, b, i: (b, hp)),
                  pl.BlockSpec((1, nb, LANES, MOBA_BLOCK), lambda hp, b, i: (b, 0, hp, 0)),
                  pl.BlockSpec((2, BIAS_TILES, MOBA_BLOCK, MOBA_BLOCK), lambda hp, b, i: (hp, 0, 0, 0))],
        out_specs=qo,
        out_shape=jax.ShapeDtypeStruct((t, ATTN_WIDTH), BF16),
        scratch_shapes=[pltpu.VMEM((nb, LANES), F32)],
        compiler_params=_cparams("arbitrary", "arbitrary", "arbitrary"),
    )(q, k, vt, bias)


MERGE_TM = 256


def _merge_kernel(x_ref, ya_ref, yg_ref, ga_ref, gg_ref, gt1_ref, sc2_ref, sh2_ref,
                  pg1_ref, g2_ref, wa_ref, wg_ref, wo_ref, wq_ref, sk0_ref, sk1_ref,
                  x1_ref, h2_ref, st_ref):
    nt = (((1,), (1,)), ((), ()))
    la = jnp.dot(ya_ref[...], wa_ref[...], preferred_element_type=F32)
    lg = jnp.dot(yg_ref[...], wg_ref[...], preferred_element_type=F32)
    merged = jax.nn.sigmoid(ga_ref[...]) * la + jax.nn.sigmoid(gg_ref[...]) * lg
    y = jnp.dot(merged.astype(BF16), wo_ref[...], preferred_element_type=F32)
    x1 = x_ref[...] + gt1_ref[0] * _rms(y, pg1_ref[...])
    x1_ref[...] = x1
    h2 = _rms(x1, g2_ref[...]) * (1.0 + sc2_ref[0]) + sh2_ref[0]
    h2_ref[...] = h2
    qp = jnp.dot(h2.astype(BF16), wq_ref[...], preferred_element_type=F32).astype(BF16)
    for h in range(PEER_HEADS):
        qh = qp[:, h * PEER_QDIM:(h + 1) * PEER_QDIM]
        st_ref[2 * h] = lax.dot_general(sk0_ref[...], qh, nt, preferred_element_type=F32)
        st_ref[2 * h + 1] = lax.dot_general(sk1_ref[...], qh, nt, preferred_element_type=F32)


def _merge(x2, ya, yg, ga, gg, gt1, sc2, sh2, post_g1, pre_g2, wa, wg, wo, wq, sk0, sk1, seq):
    t = x2.shape[0]
    tm = MERGE_TM
    per_b = seq // tm
    row = lambda w: pl.BlockSpec((tm, w), lambda i: (i, 0))
    mod = pl.BlockSpec((1, 1, D_MODEL), lambda i: (i // per_b, 0, 0))
    vec = pl.BlockSpec((1, D_MODEL), lambda i: (0, 0))
    full = lambda a: pl.BlockSpec(a.shape, lambda i: (0,) * a.ndim)
    return pl.pallas_call(
        _merge_kernel,
        grid=(t // tm,),
        in_specs=[row(D_MODEL), row(ATTN_WIDTH), row(GMLP_WIDTH), row(D_MODEL), row(D_MODEL),
                  mod, mod, mod, vec, vec, full(wa), full(wg), full(wo), full(wq), full(sk0), full(sk1)],
        out_specs=[row(D_MODEL), row(D_MODEL),
                   pl.BlockSpec((2 * PEER_HEADS, PEER_NKEYS, tm), lambda i: (0, 0, i))],
        out_shape=[jax.ShapeDtypeStruct((t, D_MODEL), F32), jax.ShapeDtypeStruct((t, D_MODEL), F32),
                   jax.ShapeDtypeStruct((2 * PEER_HEADS, PEER_NKEYS, t), F32)],
        compiler_params=_cparams("arbitrary"),
    )(x2, ya, yg, ga, gg, gt1, sc2, sh2, post_g1, pre_g2, wa, wg, wo, wq, sk0, sk1)


ROUTE_SUB = 8
_CANDS = [(a, b) for a in range(PEER_TOPK) for b in range(PEER_TOPK) if (a + 1) * (b + 1) <= PEER_TOPK]


def _route_kernel(s_ref, e_ref, g_ref, buf, buf2):
    kio = lax.broadcasted_iota(I32, (PEER_NKEYS, ROUTE_SUB, LANES), 0).astype(F32)

    def fold(pair_op, reduce_op, a):
        q = PEER_NKEYS // 4
        return reduce_op(pair_op(pair_op(a[0:q], a[q:2 * q]), pair_op(a[2 * q:3 * q], a[3 * q:4 * q])), axis=0)

    def step(buf):
        s = buf[...]
        m = fold(jnp.maximum, jnp.max, s)
        idx = fold(jnp.minimum, jnp.min, jnp.where(s == m[None], kio, float(PEER_NKEYS)))
        buf[...] = jnp.where(kio == idx[None], -jnp.inf, s)
        return m, idx

    buf[...] = s_ref[0]
    buf2[...] = s_ref[1]
    v1, i1, v2, i2 = [], [], [], []
    for _ in range(PEER_TOPK):
        m, idx = step(buf)
        v1.append(m)
        i1.append(idx)
        m, idx = step(buf2)
        v2.append(m)
        i2.append(idx)
    cand = [v1[a] + v2[b] for a, b in _CANDS]
    eid = [i1[a] * float(PEER_NKEYS) + i2[b] for a, b in _CANDS]
    flat = [float(a * PEER_TOPK + b) for a, b in _CANDS]
    sc, es = [], []
    for _ in range(PEER_TOPK):
        m = functools.reduce(jnp.maximum, cand)
        idx = functools.reduce(jnp.minimum,
                               [jnp.where(c == m, f, float(PEER_TOPK * PEER_TOPK)) for c, f in zip(cand, flat)])
        es.append(functools.reduce(jnp.maximum, [jnp.where(idx == f, e, -1.0) for e, f in zip(eid, flat)]))
        cand = [jnp.where(idx == f, -jnp.inf, c) for c, f in zip(cand, flat)]
        sc.append(m)
    ex = [jnp.exp(s - sc[0]) for s in sc]
    tot = functools.reduce(jnp.add, ex)
    for r in range(PEER_TOPK):
        e_ref[0, r] = es[r].astype(I32)
        g_ref[0, r] = ex[r] / tot


def _route(st4):
    ntile = st4.shape[2]
    grp = ROUTE_SUB
    out = pl.BlockSpec((1, PEER_TOPK, grp, LANES), lambda h, g: (h, 0, g, 0))
    shp = (PEER_HEADS, PEER_TOPK, ntile, LANES)
    return pl.pallas_call(
        _route_kernel,
        grid=(PEER_HEADS, ntile // grp),
        in_specs=[pl.BlockSpec((2, PEER_NKEYS, grp, LANES), lambda h, g: (h, 0, g, 0))],
        out_specs=[out, out],
        out_shape=[jax.ShapeDtypeStruct(shp, I32), jax.ShapeDtypeStruct(shp, F32)],
        scratch_shapes=[pltpu.VMEM((PEER_NKEYS, grp, LANES), F32)] * 2,
        compiler_params=_cparams("arbitrary", "arbitrary"),
    )(st4)


PEER_TT = 128
PEER_NBUF = 8
SUBLANES = 8
PEER_KG = PEER_PICKS // SUBLANES
PEER_RT = D_MODEL // LANES
_ORDER = (3, 2, 1, 0, 7, 6, 5, 4)


def _pack_rows(u, v):
    ub = lax.bitcast_convert_type(u.astype(BF16), jnp.uint16).astype(jnp.uint32)
    vb = lax.bitcast_convert_type(v.astype(BF16), jnp.uint16).astype(jnp.uint32)
    return (ub | (vb << 16)).reshape(u.shape[0], PEER_RT, LANES)


def _peer_kernel(e_ref, h2_ref, gt_ref, x1_ref, gt2_ref, pg2_ref, uv_ref, o_ref, *scratch):
    bufs = scratch[:PEER_NBUF]
    sem, y_sc, w_sc = scratch[PEER_NBUF:]
    tt = PEER_TT
    ahead = PEER_NBUF - 1

    def issue(t, slot):
        for k in range(PEER_PICKS):
            pltpu.make_async_copy(uv_ref.at[e_ref[t, k]], bufs[slot].at[k], sem.at[slot]
                                  ).start(priority=k % 2)

    def wait(slot):
        pltpu.make_async_copy(bufs[slot], bufs[slot], sem.at[slot]).wait()

    lane_t = lax.broadcasted_iota(I32, (PEER_PICKS, tt), 1)
    himask = jnp.uint32(0xFFFF0000)
    row = lax.broadcasted_iota(I32, (SUBLANES, LANES), 0)
    top4 = row < 4
    hi2 = (row & 2) != 0
    odd = (row & 1) != 0
    roll = lambda a, s: pltpu.roll(a, s, 0)

    def sublane_sums(a):
        b = [jnp.where(top4, a[j], a[j + 4]) + roll(jnp.where(top4, a[j + 4], a[j]), 4) for j in range(4)]
        c = [jnp.where(hi2, b[j], roll(b[j + 2], 6)) + jnp.where(hi2, roll(b[j], 2), b[j + 2]) for j in range(2)]
        return jnp.where(odd, c[0], roll(c[1], 7)) + jnp.where(odd, roll(c[0], 1), c[1])

    def compute(t, slot):
        buf = bufs[slot]
        xb = h2_ref[pl.ds(t, 1), :]
        x_tile = jnp.concatenate([xb[:, r * LANES:(r + 1) * LANES] for r in range(PEER_RT)], axis=0)
        parts = []
        for kg in range(PEER_KG):
            prods = [lax.bitcast_convert_type(buf[kg * SUBLANES + _ORDER[j]] << 16, F32) * x_tile
                     for j in range(SUBLANES)]
            parts.append(sublane_sums(prods))
        a = jnp.sum(jnp.concatenate(parts, axis=0), axis=-1, keepdims=True)
        gcol = jnp.sum(jnp.where(lane_t == t, gt_ref[...], 0.0), axis=-1, keepdims=True)
        w_sc[...] = jnp.broadcast_to(gcol * jax.nn.gelu(a), (PEER_PICKS, LANES))
        accs = [None] * 4
        for k in range(PEER_PICKS):
            term = w_sc[pl.ds(k, 1), :] * lax.bitcast_convert_type(buf[k] & himask, F32)
            accs[k % 4] = term if accs[k % 4] is None else accs[k % 4] + term
        y_tile = (accs[0] + accs[1]) + (accs[2] + accs[3])
        y_sc[pl.ds(t, 1), :] = jnp.concatenate([y_tile[s:s + 1] for s in range(SUBLANES)], axis=-1)

    for t in range(ahead):
        issue(t, t)

    def group(g, c):
        for s in range(PEER_NBUF):
            t = g * PEER_NBUF + s
            issue(t + ahead, (s + ahead) % PEER_NBUF)
            wait(s)
            compute(t, s)
        return c

    ngroups = tt // PEER_NBUF
    lax.fori_loop(0, ngroups - 1, group, 0)
    for s in range(PEER_NBUF):
        t = (ngroups - 1) * PEER_NBUF + s
        if s == 0:
            issue(t + ahead, ahead)
        wait(s)
        compute(t, s)
    o_ref[...] = x1_ref[...] + gt2_ref[0] * _rms(y_sc[...], pg2_ref[...])


def _peer(e_t, h2, g_t, x1, gt2, post_g2, uv, seq):
    t = h2.shape[0]
    tt = PEER_TT
    per_b = seq // tt
    row = pl.BlockSpec((tt, D_MODEL), lambda i: (i, 0))
    return pl.pallas_call(
        _peer_kernel,
        grid=(t // tt,),
        in_specs=[pl.BlockSpec((tt, PEER_PICKS), lambda i: (i, 0), memory_space=pltpu.SMEM),
                  row,
                  pl.BlockSpec((PEER_PICKS, tt), lambda i: (0, i)),
                  row,
                  pl.BlockSpec((1, 1, D_MODEL), lambda i: (i // per_b, 0, 0)),
                  pl.BlockSpec((1, D_MODEL), lambda i: (0, 0)),
                  pl.BlockSpec(memory_space=pl.ANY)],
        out_specs=row,
        out_shape=jax.ShapeDtypeStruct((t, D_MODEL), F32),
        scratch_shapes=[pltpu.VMEM((PEER_PICKS, PEER_RT, LANES), jnp.uint32) for _ in range(PEER_NBUF)]
        + [pltpu.SemaphoreType.DMA((PEER_NBUF,)), pltpu.VMEM((tt, D_MODEL), F32),
           pltpu.VMEM((PEER_PICKS, LANES), F32)],
        compiler_params=_cparams("arbitrary"),
    )(e_t, h2, g_t, x1, gt2, post_g2, uv)


def kernel(x, c, ada_w, ada_b, pre_g1, post_g1, w_in, gmlp_ln_g, gmlp_ln_b, gmlp_w_s, gmlp_b_s,
           rel_bias, w_branch_attn, w_branch_gmlp, w_out, pre_g2, post_g2, peer_w_query,
           peer_sub_keys, peer_u, peer_v):
    batch, seq, d = x.shape
    depth = ada_w.shape[0]
    t = batch * seq
    assert d == D_MODEL and batch <= 8 and seq % (MOBA_GROUP * MOBA_BLOCK) == 0 and t % (ROUTE_SUB * LANES) == 0

    c8 = jnp.zeros((8, d), F32).at[:batch].set(c)
    bias = _bias_tiles(rel_bias)
    x2 = x.reshape(t, d)
    for l in range(depth):
        mod = _ada(c8, ada_w[l], ada_b[l][None, :])[:batch]
        sh1, sc1, gt1, sh2, sc2, gt2 = [m[:, None, :] for m in jnp.split(mod, 6, axis=-1)]

        w_in_bf = w_in[l].astype(BF16)
        wvt = w_in_bf[:, 2 * ATTN_WIDTH:3 * ATTN_WIDTH].T
        q, k, v, zu, zv, ga, gg = _inproj(x2, sc1, sh1, pre_g1[l][None, :], w_in_bf, wvt, batch, seq)
        yg = _gmlp(zu, zv, gmlp_ln_g[l][None, :], gmlp_ln_b[l][None, :], gmlp_w_s[l], gmlp_b_s[l].T)
        ya = _moba(q, k, v, bias, batch, seq)

        zpad = jnp.zeros((PEER_NKEYS, PEER_HALF), F32)
        sk0 = jnp.concatenate([peer_sub_keys[l, 0], zpad], axis=1).astype(BF16)
        sk1 = jnp.concatenate([zpad, peer_sub_keys[l, 1]], axis=1).astype(BF16)
        x1, h2, st = _merge(x2, ya, yg, ga, gg, gt1, sc2, sh2, post_g1[l][None, :], pre_g2[l][None, :],
                            w_branch_attn[l].astype(BF16), w_branch_gmlp[l].astype(BF16),
                            w_out[l].astype(BF16), peer_w_query[l].astype(BF16), sk0, sk1, seq)

        e4, g4 = _route(st.reshape(2 * PEER_HEADS, PEER_NKEYS, t // LANES, LANES))
        e_t = e4.reshape(PEER_PICKS, t).T
        g_t = g4.reshape(PEER_PICKS, t)
        uv = _pack_rows(peer_u[l], peer_v[l])
        x2 = _peer(e_t, h2, g_t, x1, gt2, post_g2[l][None, :], uv, seq)
    return x2.reshape(batch, seq, d)
```

```python
import functools
import math

import numpy as np
import jax
import jax.numpy as jnp
from jax import lax
from jax.experimental import pallas as pl
from jax.experimental.pallas import tpu as pltpu

F32 = jnp.float32
BF16 = jnp.bfloat16
I32 = jnp.int32

D_MODEL = 1024
ATTN_HEADS = 8
HEAD_DIM = 64
ATTN_WIDTH = ATTN_HEADS * HEAD_DIM
MOBA_BLOCK = 256
MOBA_TOPK = 3
REL_BUCKETS = 32
REL_MAX_DISTANCE = 2048
GMLP_WIDTH = D_MODEL // 2
GMLP_GROUPS = 8
GMLP_GROUP_DIM = GMLP_WIDTH // GMLP_GROUPS
GMLP_CHUNK = 128
PEER_HEADS = 8
PEER_NKEYS = 128
PEER_QDIM = 128
PEER_HALF = PEER_QDIM // 2
PEER_TOPK = 16
PEER_PICKS = PEER_HEADS * PEER_TOPK
IN_WIDTH = 3 * ATTN_WIDTH + 2 * GMLP_WIDTH + 2 * D_MODEL
NORM_EPS = 1e-6
NEG_INF = -1e30

LANES = 128
BIAS_TILES = 10
VMEM_LIMIT = 56 * 1024 * 1024


def _cparams(*sem):
    return pltpu.CompilerParams(dimension_semantics=sem, vmem_limit_bytes=VMEM_LIMIT)


def _rms(xf, g):
    return xf * lax.rsqrt(jnp.mean(xf * xf, axis=-1, keepdims=True) + NORM_EPS) * g


def _ada_kernel(c_ref, w_ref, b_ref, o_ref):
    c = c_ref[...]
    cond = c * jax.nn.sigmoid(c)
    o_ref[...] = jnp.dot(cond, w_ref[...], preferred_element_type=F32,
                         precision=lax.Precision.HIGHEST) + b_ref[...]


def _ada(c8, w, b):
    n = w.shape[1]
    tn = 768
    return pl.pallas_call(
        _ada_kernel,
        grid=(n // tn,),
        in_specs=[pl.BlockSpec((8, D_MODEL), lambda j: (0, 0)),
                  pl.BlockSpec((D_MODEL, tn), lambda j: (0, j)),
                  pl.BlockSpec((1, tn), lambda j: (0, j))],
        out_specs=pl.BlockSpec((8, tn), lambda j: (0, j)),
        out_shape=jax.ShapeDtypeStruct((8, n), F32),
        compiler_params=_cparams("arbitrary"),
    )(c8, w, b)


def _bucket_table():
    max_exact = REL_BUCKETS // 2
    d = np.arange(BIAS_TILES)[:, None, None]
    c = np.arange(MOBA_BLOCK)[None, :, None]
    r = np.arange(MOBA_BLOCK)[None, None, :]
    n = np.maximum(d * MOBA_BLOCK + r - c, 0)
    nf = np.maximum(n, max_exact).astype(np.float32)
    large = max_exact + (np.log(nf / np.float32(max_exact)) / np.float32(math.log(REL_MAX_DISTANCE / max_exact))
                         * np.float32(REL_BUCKETS - max_exact)).astype(np.int32)
    large = np.minimum(large, REL_BUCKETS - 1)
    return np.where(n < max_exact, n, large).astype(np.int32)


def _bias_kernel(rb_ref, bucket_ref, o_ref):
    h = pl.program_id(0)
    for d in range(BIAS_TILES):
        bk = bucket_ref[d]
        acc = jnp.zeros(bk.shape, F32)
        for b in range(REL_BUCKETS):
            acc = jnp.where(bk == b, rb_ref[b, h], acc)
        if d == 0:
            key = lax.broadcasted_iota(I32, bk.shape, 0)
            qry = lax.broadcasted_iota(I32, bk.shape, 1)
            acc = jnp.where(qry >= key, acc, NEG_INF)
        o_ref[0, d] = acc


def _bias_tiles(rel_bias):
    bucket = jnp.asarray(_bucket_table())
    blk = (BIAS_TILES, MOBA_BLOCK, MOBA_BLOCK)
    return pl.pallas_call(
        _bias_kernel,
        grid=(ATTN_HEADS,),
        in_specs=[pl.BlockSpec(memory_space=pltpu.SMEM),
                  pl.BlockSpec(blk, lambda h: (0, 0, 0))],
        out_specs=pl.BlockSpec((1,) + blk, lambda h: (h, 0, 0, 0)),
        out_shape=jax.ShapeDtypeStruct((ATTN_HEADS,) + blk, F32),
        compiler_params=_cparams("arbitrary"),
    )(rel_bias, bucket)


IN_TM = MOBA_BLOCK
IN_NCHUNK = 512


def _inproj_kernel(x_ref, sc_ref, sh_ref, g_ref, w_ref, wvt_ref,
                   q_ref, k_ref, vt_ref, zu_ref, zv_ref, ga_ref, gg_ref):
    x = x_ref[...]
    h = _rms(x, g_ref[...]) * (1.0 + sc_ref[0]) + sh_ref[0]
    hb = h.astype(BF16)
    nt = (((1,), (1,)), ((), ()))
    vt_ref[0, 0] = lax.dot_general(wvt_ref[...], hb, nt, preferred_element_type=F32).astype(vt_ref.dtype)
    outs = ((q_ref, ATTN_WIDTH), (k_ref, ATTN_WIDTH), (None, ATTN_WIDTH),
            (zu_ref, GMLP_WIDTH), (zv_ref, GMLP_WIDTH), (ga_ref, D_MODEL), (gg_ref, D_MODEL))
    off = 0
    for ref, width in outs:
        for c0 in range(0, width if ref is not None else 0, IN_NCHUNK):
            p = jnp.dot(hb, w_ref[:, off + c0:off + c0 + IN_NCHUNK], preferred_element_type=F32)
            ref[:, c0:c0 + IN_NCHUNK] = p.astype(ref.dtype)
        off += width


def _inproj(x2, sc1, sh1, pre_g, w_in_bf, wvt_bf, batch, seq):
    t = x2.shape[0]
    tm = IN_TM
    per_b = seq // tm
    row = lambda w: pl.BlockSpec((tm, w), lambda i: (i, 0))
    mod = pl.BlockSpec((1, 1, D_MODEL), lambda i: (i // per_b, 0, 0))
    return pl.pallas_call(
        _inproj_kernel,
        grid=(t // tm,),
        in_specs=[row(D_MODEL), mod, mod,
                  pl.BlockSpec((1, D_MODEL), lambda i: (0, 0)),
                  pl.BlockSpec((D_MODEL, IN_WIDTH), lambda i: (0, 0)),
                  pl.BlockSpec((ATTN_WIDTH, D_MODEL), lambda i: (0, 0))],
        out_specs=[row(ATTN_WIDTH), row(ATTN_WIDTH),
                   pl.BlockSpec((1, 1, ATTN_WIDTH, tm), lambda i: (i // per_b, i % per_b, 0, 0)),
                   row(GMLP_WIDTH), row(GMLP_WIDTH), row(D_MODEL), row(D_MODEL)],
        out_shape=[jax.ShapeDtypeStruct((t, ATTN_WIDTH), BF16)] * 2
        + [jax.ShapeDtypeStruct((batch, per_b, ATTN_WIDTH, tm), BF16)]
        + [jax.ShapeDtypeStruct((t, GMLP_WIDTH), F32)] * 2
        + [jax.ShapeDtypeStruct((t, D_MODEL), F32)] * 2,
        compiler_params=_cparams("arbitrary"),
    )(x2, sc1, sh1, pre_g, w_in_bf, wvt_bf)


GMLP_TM = 512


def _gmlp_kernel(zu_ref, zv_ref, lg_ref, lb_ref, ws_ref, bs_ref, y_ref):
    u = jax.nn.gelu(zu_ref[...])
    gv = jax.nn.gelu(zv_ref[...])
    mu = jnp.mean(gv, axis=-1, keepdims=True)
    vc = gv - mu
    var = jnp.mean(vc * vc, axis=-1, keepdims=True)
    v = (vc * lax.rsqrt(var + NORM_EPS) * lg_ref[...] + lb_ref[...]).astype(BF16)
    tri = (lax.broadcasted_iota(I32, (GMLP_CHUNK, GMLP_CHUNK), 0)
           >= lax.broadcasted_iota(I32, (GMLP_CHUNK, GMLP_CHUNK), 1))
    lane = lax.broadcasted_iota(I32, (GMLP_CHUNK, LANES), 1)
    lo = lane < GMLP_GROUP_DIM
    zero = jnp.zeros((), BF16)
    for pair in range(GMLP_GROUPS // 2):
        w0 = jnp.where(tri, ws_ref[2 * pair], 0.0).astype(BF16)
        w1 = jnp.where(tri, ws_ref[2 * pair + 1], 0.0).astype(BF16)
        b0 = bs_ref[:, 2 * pair:2 * pair + 1]
        b1 = bs_ref[:, 2 * pair + 1:2 * pair + 2]
        bias = jnp.where(lo, b0, b1)
        cols = slice(pair * LANES, (pair + 1) * LANES)
        for ch in range(GMLP_TM // GMLP_CHUNK):
            rows = slice(ch * GMLP_CHUNK, (ch + 1) * GMLP_CHUNK)
            v2 = v[rows, cols]
            mixed = (jnp.dot(w0, jnp.where(lo, v2, zero), preferred_element_type=F32)
                     + jnp.dot(w1, jnp.where(lo, zero, v2), preferred_element_type=F32) + bias)
            y_ref[rows, cols] = (u[rows, cols] * mixed).astype(y_ref.dtype)


def _gmlp(zu, zv, ln_g, ln_b, w_s, bs_t):
    t = zu.shape[0]
    tm = GMLP_TM
    row = pl.BlockSpec((tm, GMLP_WIDTH), lambda i: (i, 0))
    vec = pl.BlockSpec((1, GMLP_WIDTH), lambda i: (0, 0))
    return pl.pallas_call(
        _gmlp_kernel,
        grid=(t // tm,),
        in_specs=[row, row, vec, vec,
                  pl.BlockSpec((GMLP_GROUPS, GMLP_CHUNK, GMLP_CHUNK), lambda i: (0, 0, 0)),
                  pl.BlockSpec((GMLP_CHUNK, GMLP_GROUPS), lambda i: (0, 0))],
        out_specs=row,
        out_shape=jax.ShapeDtypeStruct((t, GMLP_WIDTH), BF16),
        compiler_params=_cparams("arbitrary"),
    )(zu, zv, ln_g, ln_b, w_s, bs_t)


MOBA_GROUP = 4


def _moba_kernel(q_ref, k_ref, vt_ref, bias_ref, o_ref, kmean_sc, *, nb, n_slots):
    i = pl.program_id(2)
    blk = MOBA_BLOCK
    scale = HEAD_DIM ** -0.5
    nt = (((1,), (1,)), ((), ()))

    @pl.when(i == 0)
    def _():
        def body(j, c):
            kb = k_ref[pl.ds(pl.multiple_of(j * blk, blk), blk), :].astype(F32)
            kmean_sc[pl.ds(j, 1), :] = jnp.mean(kb, axis=0, keepdims=True)
            return c
        lax.fori_loop(0, nb, body, 0)

    q2 = q_ref[...]
    km = kmean_sc[...].astype(BF16)
    lane_q = lax.broadcasted_iota(I32, (blk, LANES), 1)
    blk_id = lax.broadcasted_iota(I32, (nb, blk), 0).astype(F32)
    i_f = i.astype(F32)
    zero = jnp.zeros((), BF16)

    qs, sels = [], []
    for hh in range(2):
        head_lanes = (lane_q < HEAD_DIM) if hh == 0 else (lane_q >= HEAD_DIM)
        qh = jnp.where(head_lanes, q2, zero)
        qs.append((qh.astype(F32) * scale).astype(BF16))
        score = lax.dot_general(km, qh, nt, preferred_element_type=F32)
        score = jnp.where(blk_id < i_f, score, NEG_INF)
        sel = []
        for r in range(n_slots):
            m = jnp.max(score, axis=0, keepdims=True)
            idx = jnp.min(jnp.where(score == m, blk_id, float(nb)), axis=0, keepdims=True)
            sel.append(jnp.where(i > r, idx, -1.0))
            score = jnp.where(blk_id == idx, -jnp.inf, score)
        sels.append(sel)

    def tile(jb, hh):
        jb_f = jb.astype(F32)
        kb = k_ref[pl.ds(pl.multiple_of(jb * blk, blk), blk), :]
        dist = jnp.clip(i - jb, 0, BIAS_TILES - 1)
        hit = jb_f == i_f
        for sr in sels[hh]:
            hit = hit | (sr == jb_f)
        s = lax.dot_general(kb, qs[hh], nt, preferred_element_type=F32) + bias_ref[hh, dist]
        return s, hit, jnp.where(hit, jnp.max(s, axis=0, keepdims=True), NEG_INF)

    grp = MOBA_GROUP

    def body(step, carry):
        first = (i // grp - step) * grp
        new = []
        for hh in range(2):
            m_run, l_run, acc = carry[hh]
            tiles = [tile(first + g, hh) for g in range(grp)]
            m_new = functools.reduce(jnp.maximum, [m_run] + [tl[2] for tl in tiles])
            l_new = jnp.exp(m_run - m_new) * l_run
            acc_new = jnp.exp(m_run - m_new) * acc
            for g, (s, hit, _) in enumerate(tiles):
                p = jnp.exp(s - jnp.where(hit, m_new, m_new - NEG_INF))
                l_new = l_new + jnp.sum(p, axis=0, keepdims=True)
                acc_new = acc_new + jnp.dot(vt_ref[0, first + g], p.astype(BF16), preferred_element_type=F32)
            new.append((m_new, l_new, acc_new))
        return tuple(new)

    init = tuple((jnp.full((1, blk), NEG_INF, F32), jnp.zeros((1, blk), F32), jnp.zeros((LANES, blk), F32))
                 for _ in range(2))
    fin = lax.fori_loop(0, i // grp + 1, body, init)
    o0 = fin[0][2] / fin[0][1]
    o1 = fin[1][2] / fin[1][1]
    out_t = jnp.concatenate([o0[:HEAD_DIM], o1[HEAD_DIM:]], axis=0)
    o_ref[...] = out_t.T.astype(o_ref.dtype)


def _moba(q, k, vt, bias, batch, seq):
    t = q.shape[0]
    nb = seq // MOBA_BLOCK
    n_slots = min(MOBA_TOPK, nb - 1)
    pairs = ATTN_HEADS // 2
    qo = pl.BlockSpec((MOBA_BLOCK, LANES), lambda hp, b, i: (b * nb + i, hp))
    return pl.pallas_call(
        functools.partial(_moba_kernel, nb=nb, n_slots=n_slots),
        grid=(pairs, batch, nb),
        in_specs=[qo,
                  pl.BlockSpec((seq, LANES), lambda hp, b, i: (b, hp)),
                  pl.BlockSpec((1, nb, LANES, MOBA_BLOCK), lambda hp, b, i: (b, 0, hp, 0)),
                  pl.BlockSpec((2, BIAS_TILES, MOBA_BLOCK, MOBA_BLOCK), lambda hp, b, i: (hp, 0, 0, 0))],
        out_specs=qo,
        out_shape=jax.ShapeDtypeStruct((t, ATTN_WIDTH), BF16),
        scratch_shapes=[pltpu.VMEM((nb, LANES), F32)],
        compiler_params=_cparams("arbitrary", "arbitrary", "arbitrary"),
    )(q, k, vt, bias)


MERGE_TM = 256


def _merge_kernel(x_ref, ya_ref, yg_ref, ga_ref, gg_ref, gt1_ref, sc2_ref, sh2_ref,
                  pg1_ref, g2_ref, wa_ref, wg_ref, wo_ref, wq_ref, sk0_ref, sk1_ref,
                  x1_ref, h2_ref, st_ref):
    nt = (((1,), (1,)), ((), ()))
    la = jnp.dot(ya_ref[...], wa_ref[...], preferred_element_type=F32)
    lg = jnp.dot(yg_ref[...], wg_ref[...], preferred_element_type=F32)
    merged = jax.nn.sigmoid(ga_ref[...]) * la + jax.nn.sigmoid(gg_ref[...]) * lg
    y = jnp.dot(merged.astype(BF16), wo_ref[...], preferred_element_type=F32)
    x1 = x_ref[...] + gt1_ref[0] * _rms(y, pg1_ref[...])
    x1_ref[...] = x1
    h2 = _rms(x1, g2_ref[...]) * (1.0 + sc2_ref[0]) + sh2_ref[0]
    h2_ref[...] = h2
    qp = jnp.dot(h2.astype(BF16), wq_ref[...], preferred_element_type=F32).astype(BF16)
    for h in range(PEER_HEADS):
        qh = qp[:, h * PEER_QDIM:(h + 1) * PEER_QDIM]
        st_ref[2 * h] = lax.dot_general(sk0_ref[...], qh, nt, preferred_element_type=F32)
        st_ref[2 * h + 1] = lax.dot_general(sk1_ref[...], qh, nt, preferred_element_type=F32)


def _merge(x2, ya, yg, ga, gg, gt1, sc2, sh2, post_g1, pre_g2, wa, wg, wo, wq, sk0, sk1, seq):
    t = x2.shape[0]
    tm = MERGE_TM
    per_b = seq // tm
    row = lambda w: pl.BlockSpec((tm, w), lambda i: (i, 0))
    mod = pl.BlockSpec((1, 1, D_MODEL), lambda i: (i // per_b, 0, 0))
    vec = pl.BlockSpec((1, D_MODEL), lambda i: (0, 0))
    full = lambda a: pl.BlockSpec(a.shape, lambda i: (0,) * a.ndim)
    return pl.pallas_call(
        _merge_kernel,
        grid=(t // tm,),
        in_specs=[row(D_MODEL), row(ATTN_WIDTH), row(GMLP_WIDTH), row(D_MODEL), row(D_MODEL),
                  mod, mod, mod, vec, vec, full(wa), full(wg), full(wo), full(wq), full(sk0), full(sk1)],
        out_specs=[row(D_MODEL), row(D_MODEL),
                   pl.BlockSpec((2 * PEER_HEADS, PEER_NKEYS, tm), lambda i: (0, 0, i))],
        out_shape=[jax.ShapeDtypeStruct((t, D_MODEL), F32), jax.ShapeDtypeStruct((t, D_MODEL), F32),
                   jax.ShapeDtypeStruct((2 * PEER_HEADS, PEER_NKEYS, t), F32)],
        compiler_params=_cparams("arbitrary"),
    )(x2, ya, yg, ga, gg, gt1, sc2, sh2, post_g1, pre_g2, wa, wg, wo, wq, sk0, sk1)


ROUTE_SUB = 8
_CANDS = [(a, b) for a in range(PEER_TOPK) for b in range(PEER_TOPK) if (a + 1) * (b + 1) <= PEER_TOPK]


def _route_kernel(s_ref, e_ref, g_ref, buf, buf2):
    kio = lax.broadcasted_iota(I32, (PEER_NKEYS, ROUTE_SUB, LANES), 0).astype(F32)

    def fold(pair_op, reduce_op, a):
        q = PEER_NKEYS // 4
        return reduce_op(pair_op(pair_op(a[0:q], a[q:2 * q]), pair_op(a[2 * q:3 * q], a[3 * q:4 * q])), axis=0)

    def step(buf):
        s = buf[...]
        m = fold(jnp.maximum, jnp.max, s)
        idx = fold(jnp.minimum, jnp.min, jnp.where(s == m[None], kio, float(PEER_NKEYS)))
        buf[...] = jnp.where(kio == idx[None], -jnp.inf, s)
        return m, idx

    buf[...] = s_ref[0]
    buf2[...] = s_ref[1]
    v1, i1, v2, i2 = [], [], [], []
    for _ in range(PEER_TOPK):
        m, idx = step(buf)
        v1.append(m)
        i1.append(idx)
        m, idx = step(buf2)
        v2.append(m)
        i2.append(idx)
    cand = [v1[a] + v2[b] for a, b in _CANDS]
    eid = [i1[a] * float(PEER_NKEYS) + i2[b] for a, b in _CANDS]
    flat = [float(a * PEER_TOPK + b) for a, b in _CANDS]
    sc, es = [], []
    for _ in range(PEER_TOPK):
        m = functools.reduce(jnp.maximum, cand)
        idx = functools.reduce(jnp.minimum,
                               [jnp.where(c == m, f, float(PEER_TOPK * PEER_TOPK)) for c, f in zip(cand, flat)])
        es.append(functools.reduce(jnp.maximum, [jnp.where(idx == f, e, -1.0) for e, f in zip(eid, flat)]))
        cand = [jnp.where(idx == f, -jnp.inf, c) for c, f in zip(cand, flat)]
        sc.append(m)
    ex = [jnp.exp(s - sc[0]) for s in sc]
    tot = functools.reduce(jnp.add, ex)
    for r in range(PEER_TOPK):
        e_ref[0, r] = es[r].astype(I32)
        g_ref[0, r] = ex[r] / tot


def _route(st4):
    ntile = st4.shape[2]
    grp = ROUTE_SUB
    out = pl.BlockSpec((1, PEER_TOPK, grp, LANES), lambda h, g: (h, 0, g, 0))
    shp = (PEER_HEADS, PEER_TOPK, ntile, LANES)
    return pl.pallas_call(
        _route_kernel,
        grid=(PEER_HEADS, ntile // grp),
        in_specs=[pl.BlockSpec((2, PEER_NKEYS, grp, LANES), lambda h, g: (h, 0, g, 0))],
        out_specs=[out, out],
        out_shape=[jax.ShapeDtypeStruct(shp, I32), jax.ShapeDtypeStruct(shp, F32)],
        scratch_shapes=[pltpu.VMEM((PEER_NKEYS, grp, LANES), F32)] * 2,
        compiler_params=_cparams("arbitrary", "arbitrary"),
    )(st4)


PEER_TT = 256
PEER_NBUF = 8
SUBLANES = 8
PEER_KG = PEER_PICKS // SUBLANES
PEER_RT = D_MODEL // LANES
_ORDER = (3, 2, 1, 0, 7, 6, 5, 4)
PEER_F1 = 6
PEER_F2 = PEER_PICKS // (PEER_PICKS - PEER_KG * PEER_F1)


def _pack_rows(u, v):
    ub = lax.bitcast_convert_type(u.astype(BF16), jnp.uint16).astype(jnp.uint32)
    vb = lax.bitcast_convert_type(v.astype(BF16), jnp.uint16).astype(jnp.uint32)
    return (ub | (vb << 16)).reshape(u.shape[0], PEER_RT, LANES)


def _peer_kernel(e_ref, h2_ref, gt_ref, x1_ref, gt2_ref, pg2_ref, uv_ref, o_ref, *scratch):
    bufs = scratch[:PEER_NBUF]
    sem, y_sc, w_sc = scratch[PEER_NBUF:]
    tt = PEER_TT
    ahead = PEER_NBUF - 1

    def issue(t, slot, picks=range(PEER_PICKS)):
        for k in picks:
            pltpu.make_async_copy(uv_ref.at[e_ref[t, k]], bufs[slot].at[k], sem.at[slot]
                                  ).start(priority=k % 2)

    def wait(slot):
        pltpu.make_async_copy(bufs[slot], bufs[slot], sem.at[slot]).wait()

    lane_t = lax.broadcasted_iota(I32, (PEER_PICKS, tt), 1)
    himask = jnp.uint32(0xFFFF0000)
    row = lax.broadcasted_iota(I32, (SUBLANES, LANES), 0)
    top4 = row < 4
    hi2 = (row & 2) != 0
    odd = (row & 1) != 0
    roll = lambda a, s: pltpu.roll(a, s, 0)

    def sublane_sums(a):
        b = [jnp.where(top4, a[j], a[j + 4]) + roll(jnp.where(top4, a[j + 4], a[j]), 4) for j in range(4)]
        c = [jnp.where(hi2, b[j], roll(b[j + 2], 6)) + jnp.where(hi2, roll(b[j], 2), b[j + 2]) for j in range(2)]
        return jnp.where(odd, c[0], roll(c[1], 7)) + jnp.where(odd, roll(c[0], 1), c[1])

    def compute(t, slot, fetch=None):
        buf = bufs[slot]
        xb = h2_ref[pl.ds(t, 1), :]
        x_tile = jnp.concatenate([xb[:, r * LANES:(r + 1) * LANES] for r in range(PEER_RT)], axis=0)
        parts = []
        for kg in range(PEER_KG):
            prods = [lax.bitcast_convert_type(buf[kg * SUBLANES + _ORDER[j]] << 16, F32) * x_tile
                     for j in range(SUBLANES)]
            parts.append(sublane_sums(prods))
            if fetch is not None:
                fetch(range(kg * PEER_F1, (kg + 1) * PEER_F1))
        a = jnp.sum(jnp.concatenate(parts, axis=0), axis=-1, keepdims=True)
        gcol = jnp.sum(jnp.where(lane_t == t, gt_ref[...], 0.0), axis=-1, keepdims=True)
        w_sc[...] = jnp.broadcast_to(gcol * jax.nn.gelu(a), (PEER_PICKS, LANES))
        accs = [None] * 4
        for k in range(PEER_PICKS):
            term = w_sc[pl.ds(k, 1), :] * lax.bitcast_convert_type(buf[k] & himask, F32)
            accs[k % 4] = term if accs[k % 4] is None else accs[k % 4] + term
            if fetch is not None and k % PEER_F2 == 0:
                fetch((PEER_KG * PEER_F1 + k // PEER_F2,))
        y_tile = (accs[0] + accs[1]) + (accs[2] + accs[3])
        y_sc[pl.ds(t, 1), :] = jnp.concatenate([y_tile[s:s + 1] for s in range(SUBLANES)], axis=-1)

    for t in range(ahead):
        issue(t, t)

    def group(g, c):
        for s in range(PEER_NBUF):
            t = g * PEER_NBUF + s
            wait(s)
            compute(t, s, functools.partial(issue, t + ahead, (s + ahead) % PEER_NBUF))
        return c

    ngroups = tt // PEER_NBUF
    lax.fori_loop(0, ngroups - 1, group, 0)
    for s in range(PEER_NBUF):
        t = (ngroups - 1) * PEER_NBUF + s
        if s == 0:
            issue(t + ahead, ahead)
        wait(s)
        compute(t, s)
    o_ref[...] = x1_ref[...] + gt2_ref[0] * _rms(y_sc[...], pg2_ref[...])


def _peer(e_t, h2, g_t, x1, gt2, post_g2, uv, seq):
    t = h2.shape[0]
    tt = PEER_TT
    per_b = seq // tt
    row = pl.BlockSpec((tt, D_MODEL), lambda i: (i, 0))
    return pl.pallas_call(
        _peer_kernel,
        grid=(t // tt,),
        in_specs=[pl.BlockSpec((tt, PEER_PICKS), lambda i: (i, 0), memory_space=pltpu.SMEM),
                  row,
                  pl.BlockSpec((PEER_PICKS, tt), lambda i: (0, i)),
                  row,
                  pl.BlockSpec((1, 1, D_MODEL), lambda i: (i // per_b, 0, 0)),
                  pl.BlockSpec((1, D_MODEL), lambda i: (0, 0)),
                  pl.BlockSpec(memory_space=pl.ANY)],
        out_specs=row,
        out_shape=jax.ShapeDtypeStruct((t, D_MODEL), F32),
        scratch_shapes=[pltpu.VMEM((PEER_PICKS, PEER_RT, LANES), jnp.uint32) for _ in range(PEER_NBUF)]
        + [pltpu.SemaphoreType.DMA((PEER_NBUF,)), pltpu.VMEM((tt, D_MODEL), F32),
           pltpu.VMEM((PEER_PICKS, LANES), F32)],
        compiler_params=_cparams("arbitrary"),
    )(e_t, h2, g_t, x1, gt2, post_g2, uv)


def kernel(x, c, ada_w, ada_b, pre_g1, post_g1, w_in, gmlp_ln_g, gmlp_ln_b, gmlp_w_s, gmlp_b_s,
           rel_bias, w_branch_attn, w_branch_gmlp, w_out, pre_g2, post_g2, peer_w_query,
           peer_sub_keys, peer_u, peer_v):
    batch, seq, d = x.shape
    depth = ada_w.shape[0]
    t = batch * seq
    assert d == D_MODEL and batch <= 8 and seq % (MOBA_GROUP * MOBA_BLOCK) == 0 and t % (ROUTE_SUB * LANES) == 0

    c8 = jnp.zeros((8, d), F32).at[:batch].set(c)
    bias = _bias_tiles(rel_bias)
    x2 = x.reshape(t, d)
    for l in range(depth):
        mod = _ada(c8, ada_w[l], ada_b[l][None, :])[:batch]
        sh1, sc1, gt1, sh2, sc2, gt2 = [m[:, None, :] for m in jnp.split(mod, 6, axis=-1)]

        w_in_bf = w_in[l].astype(BF16)
        wvt = w_in_bf[:, 2 * ATTN_WIDTH:3 * ATTN_WIDTH].T
        q, k, v, zu, zv, ga, gg = _inproj(x2, sc1, sh1, pre_g1[l][None, :], w_in_bf, wvt, batch, seq)
        yg = _gmlp(zu, zv, gmlp_ln_g[l][None, :], gmlp_ln_b[l][None, :], gmlp_w_s[l], gmlp_b_s[l].T)
        ya = _moba(q, k, v, bias, batch, seq)

        zpad = jnp.zeros((PEER_NKEYS, PEER_HALF), F32)
        sk0 = jnp.concatenate([peer_sub_keys[l, 0], zpad], axis=1).astype(BF16)
        sk1 = jnp.concatenate([zpad, peer_sub_keys[l, 1]], axis=1).astype(BF16)
        x1, h2, st = _merge(x2, ya, yg, ga, gg, gt1, sc2, sh2, post_g1[l][None, :], pre_g2[l][None, :],
                            w_branch_attn[l].astype(BF16), w_branch_gmlp[l].astype(BF16),
                            w_out[l].astype(BF16), peer_w_query[l].astype(BF16), sk0, sk1, seq)

        e4, g4 = _route(st.reshape(2 * PEER_HEADS, PEER_NKEYS, t // LANES, LANES))
        e_t = e4.reshape(PEER_PICKS, t).T
        g_t = g4.reshape(PEER_PICKS, t)
        uv = _pack_rows(peer_u[l], peer_v[l])
        x2 = _peer(e_t, h2, g_t, x1, gt2, post_g2[l][None, :], uv, seq)
    return x2.reshape(batch, seq, d)
```

```python
import functools
import math

import numpy as np
import jax
import jax.numpy as jnp
from jax import lax
from jax.experimental import pallas as pl
from jax.experimental.pallas import tpu as pltpu

F32 = jnp.float32
BF16 = jnp.bfloat16
I32 = jnp.int32

D_MODEL = 1024
ATTN_HEADS = 8
HEAD_DIM = 64
ATTN_WIDTH = ATTN_HEADS * HEAD_DIM
MOBA_BLOCK = 256
MOBA_TOPK = 3
REL_BUCKETS = 32
REL_MAX_DISTANCE = 2048
GMLP_WIDTH = D_MODEL // 2
GMLP_GROUPS = 8
GMLP_GROUP_DIM = GMLP_WIDTH // GMLP_GROUPS
GMLP_CHUNK = 128
PEER_HEADS = 8
PEER_NKEYS = 128
PEER_QDIM = 128
PEER_HALF = PEER_QDIM // 2
PEER_TOPK = 16
PEER_PICKS = PEER_HEADS * PEER_TOPK
IN_WIDTH = 3 * ATTN_WIDTH + 2 * GMLP_WIDTH + 2 * D_MODEL
NORM_EPS = 1e-6
NEG_INF = -1e30

LANES = 128
BIAS_TILES = 10
VMEM_LIMIT = 56 * 1024 * 1024


def _cparams(*sem):
    return pltpu.CompilerParams(dimension_semantics=sem, vmem_limit_bytes=VMEM_LIMIT)


def _rms(xf, g):
    return xf * lax.rsqrt(jnp.mean(xf * xf, axis=-1, keepdims=True) + NORM_EPS) * g


def _ada_kernel(c_ref, w_ref, b_ref, o_ref):
    c = c_ref[...]
    cond = c * jax.nn.sigmoid(c)
    o_ref[...] = jnp.dot(cond, w_ref[...], preferred_element_type=F32,
                         precision=lax.Precision.HIGHEST) + b_ref[...]


def _ada(c8, w, b):
    n = w.shape[1]
    tn = 768
    return pl.pallas_call(
        _ada_kernel,
        grid=(n // tn,),
        in_specs=[pl.BlockSpec((8, D_MODEL), lambda j: (0, 0)),
                  pl.BlockSpec((D_MODEL, tn), lambda j: (0, j)),
                  pl.BlockSpec((1, tn), lambda j: (0, j))],
        out_specs=pl.BlockSpec((8, tn), lambda j: (0, j)),
        out_shape=jax.ShapeDtypeStruct((8, n), F32),
        compiler_params=_cparams("arbitrary"),
    )(c8, w, b)


def _bucket_table():
    max_exact = REL_BUCKETS // 2
    d = np.arange(BIAS_TILES)[:, None, None]
    c = np.arange(MOBA_BLOCK)[None, :, None]
    r = np.arange(MOBA_BLOCK)[None, None, :]
    n = np.maximum(d * MOBA_BLOCK + r - c, 0)
    nf = np.maximum(n, max_exact).astype(np.float32)
    large = max_exact + (np.log(nf / np.float32(max_exact)) / np.float32(math.log(REL_MAX_DISTANCE / max_exact))
                         * np.float32(REL_BUCKETS - max_exact)).astype(np.int32)
    large = np.minimum(large, REL_BUCKETS - 1)
    return np.where(n < max_exact, n, large).astype(np.int32)


def _bias_kernel(rb_ref, bucket_ref, o_ref):
    h = pl.program_id(0)
    for d in range(BIAS_TILES):
        bk = bucket_ref[d]
        acc = jnp.zeros(bk.shape, F32)
        for b in range(REL_BUCKETS):
            acc = jnp.where(bk == b, rb_ref[b, h], acc)
        if d == 0:
            key = lax.broadcasted_iota(I32, bk.shape, 0)
            qry = lax.broadcasted_iota(I32, bk.shape, 1)
            acc = jnp.where(qry >= key, acc, NEG_INF)
        o_ref[0, d] = acc


def _bias_tiles(rel_bias):
    bucket = jnp.asarray(_bucket_table())
    blk = (BIAS_TILES, MOBA_BLOCK, MOBA_BLOCK)
    return pl.pallas_call(
        _bias_kernel,
        grid=(ATTN_HEADS,),
        in_specs=[pl.BlockSpec(memory_space=pltpu.SMEM),
                  pl.BlockSpec(blk, lambda h: (0, 0, 0))],
        out_specs=pl.BlockSpec((1,) + blk, lambda h: (h, 0, 0, 0)),
        out_shape=jax.ShapeDtypeStruct((ATTN_HEADS,) + blk, F32),
        compiler_params=_cparams("arbitrary"),
    )(rel_bias, bucket)


IN_TM = MOBA_BLOCK
IN_NCHUNK = 512


def _inproj_kernel(x_ref, sc_ref, sh_ref, g_ref, w_ref, wvt_ref,
                   q_ref, k_ref, vt_ref, zu_ref, zv_ref, ga_ref, gg_ref):
    x = x_ref[...]
    h = _rms(x, g_ref[...]) * (1.0 + sc_ref[0]) + sh_ref[0]
    hb = h.astype(BF16)
    nt = (((1,), (1,)), ((), ()))
    vt_ref[0, 0] = lax.dot_general(wvt_ref[...], hb, nt, preferred_element_type=F32).astype(vt_ref.dtype)
    outs = ((q_ref, ATTN_WIDTH), (k_ref, ATTN_WIDTH), (None, ATTN_WIDTH),
            (zu_ref, GMLP_WIDTH), (zv_ref, GMLP_WIDTH), (ga_ref, D_MODEL), (gg_ref, D_MODEL))
    off = 0
    for ref, width in outs:
        for c0 in range(0, width if ref is not None else 0, IN_NCHUNK):
            p = jnp.dot(hb, w_ref[:, off + c0:off + c0 + IN_NCHUNK], preferred_element_type=F32)
            ref[:, c0:c0 + IN_NCHUNK] = p.astype(ref.dtype)
        off += width


def _inproj(x2, sc1, sh1, pre_g, w_in_bf, wvt_bf, batch, seq):
    t = x2.shape[0]
    tm = IN_TM
    per_b = seq // tm
    row = lambda w: pl.BlockSpec((tm, w), lambda i: (i, 0))
    mod = pl.BlockSpec((1, 1, D_MODEL), lambda i: (i // per_b, 0, 0))
    return pl.pallas_call(
        _inproj_kernel,
        grid=(t // tm,),
        in_specs=[row(D_MODEL), mod, mod,
                  pl.BlockSpec((1, D_MODEL), lambda i: (0, 0)),
                  pl.BlockSpec((D_MODEL, IN_WIDTH), lambda i: (0, 0)),
                  pl.BlockSpec((ATTN_WIDTH, D_MODEL), lambda i: (0, 0))],
        out_specs=[row(ATTN_WIDTH), row(ATTN_WIDTH),
                   pl.BlockSpec((1, 1, ATTN_WIDTH, tm), lambda i: (i // per_b, i % per_b, 0, 0)),
                   row(GMLP_WIDTH), row(GMLP_WIDTH), row(D_MODEL), row(D_MODEL)],
        out_shape=[jax.ShapeDtypeStruct((t, ATTN_WIDTH), BF16)] * 2
        + [jax.ShapeDtypeStruct((batch, per_b, ATTN_WIDTH, tm), BF16)]
        + [jax.ShapeDtypeStruct((t, GMLP_WIDTH), F32)] * 2
        + [jax.ShapeDtypeStruct((t, D_MODEL), F32)] * 2,
        compiler_params=_cparams("arbitrary"),
    )(x2, sc1, sh1, pre_g, w_in_bf, wvt_bf)


GMLP_TM = 512


def _gmlp_kernel(zu_ref, zv_ref, lg_ref, lb_ref, ws_ref, bs_ref, y_ref):
    u = jax.nn.gelu(zu_ref[...])
    gv = jax.nn.gelu(zv_ref[...])
    mu = jnp.mean(gv, axis=-1, keepdims=True)
    vc = gv - mu
    var = jnp.mean(vc * vc, axis=-1, keepdims=True)
    v = (vc * lax.rsqrt(var + NORM_EPS) * lg_ref[...] + lb_ref[...]).astype(BF16)
    tri = (lax.broadcasted_iota(I32, (GMLP_CHUNK, GMLP_CHUNK), 0)
           >= lax.broadcasted_iota(I32, (GMLP_CHUNK, GMLP_CHUNK), 1))
    lane = lax.broadcasted_iota(I32, (GMLP_CHUNK, LANES), 1)
    lo = lane < GMLP_GROUP_DIM
    zero = jnp.zeros((), BF16)
    for pair in range(GMLP_GROUPS // 2):
        w0 = jnp.where(tri, ws_ref[2 * pair], 0.0).astype(BF16)
        w1 = jnp.where(tri, ws_ref[2 * pair + 1], 0.0).astype(BF16)
        b0 = bs_ref[:, 2 * pair:2 * pair + 1]
        b1 = bs_ref[:, 2 * pair + 1:2 * pair + 2]
        bias = jnp.where(lo, b0, b1)
        cols = slice(pair * LANES, (pair + 1) * LANES)
        for ch in range(GMLP_TM // GMLP_CHUNK):
            rows = slice(ch * GMLP_CHUNK, (ch + 1) * GMLP_CHUNK)
            v2 = v[rows, cols]
            mixed = (jnp.dot(w0, jnp.where(lo, v2, zero), preferred_element_type=F32)
                     + jnp.dot(w1, jnp.where(lo, zero, v2), preferred_element_type=F32) + bias)
            y_ref[rows, cols] = (u[rows, cols] * mixed).astype(y_ref.dtype)


def _gmlp(zu, zv, ln_g, ln_b, w_s, bs_t):
    t = zu.shape[0]
    tm = GMLP_TM
    row = pl.BlockSpec((tm, GMLP_WIDTH), lambda i: (i, 0))
    vec = pl.BlockSpec((1, GMLP_WIDTH), lambda i: (0, 0))
    return pl.pallas_call(
        _gmlp_kernel,
        grid=(t // tm,),
        in_specs=[row, row, vec, vec,
                  pl.BlockSpec((GMLP_GROUPS, GMLP_CHUNK, GMLP_CHUNK), lambda i: (0, 0, 0)),
                  pl.BlockSpec((GMLP_CHUNK, GMLP_GROUPS), lambda i: (0, 0))],
        out_specs=row,
        out_shape=jax.ShapeDtypeStruct((t, GMLP_WIDTH), BF16),
        compiler_params=_cparams("arbitrary"),
    )(zu, zv, ln_g, ln_b, w_s, bs_t)


MOBA_GROUP = 4


def _moba_kernel(q_ref, k_ref, vt_ref, bias_ref, o_ref, kmean_sc, *, nb, n_slots):
    i = pl.program_id(2)
    blk = MOBA_BLOCK
    scale = HEAD_DIM ** -0.5
    nt = (((1,), (1,)), ((), ()))

    @pl.when(i == 0)
    def _():
        def body(j, c):
            kb = k_ref[pl.ds(pl.multiple_of(j * blk, blk), blk), :].astype(F32)
            kmean_sc[pl.ds(j, 1), :] = jnp.mean(kb, axis=0, keepdims=True)
            return c
        lax.fori_loop(0, nb, body, 0)

    q2 = q_ref[...]
    km = kmean_sc[...].astype(BF16)
    lane_q = lax.broadcasted_iota(I32, (blk, LANES), 1)
    blk_id = lax.broadcasted_iota(I32, (nb, blk), 0).astype(F32)
    i_f = i.astype(F32)
    zero = jnp.zeros((), BF16)

    qs, sels = [], []
    for hh in range(2):
        head_lanes = (lane_q < HEAD_DIM) if hh == 0 else (lane_q >= HEAD_DIM)
        qh = jnp.where(head_lanes, q2, zero)
        qs.append((qh.astype(F32) * scale).astype(BF16))
        score = lax.dot_general(km, qh, nt, preferred_element_type=F32)
        score = jnp.where(blk_id < i_f, score, NEG_INF)
        sel = []
        for r in range(n_slots):
            m = jnp.max(score, axis=0, keepdims=True)
            idx = jnp.min(jnp.where(score == m, blk_id, float(nb)), axis=0, keepdims=True)
            sel.append(jnp.where(i > r, idx, -1.0))
            score = jnp.where(blk_id == idx, -jnp.inf, score)
        sels.append(sel)

    def tile(jb, hh):
        jb_f = jb.astype(F32)
        kb = k_ref[pl.ds(pl.multiple_of(jb * blk, blk), blk), :]
        dist = jnp.clip(i - jb, 0, BIAS_TILES - 1)
        hit = jb_f == i_f
        for sr in sels[hh]:
            hit = hit | (sr == jb_f)
        s = lax.dot_general(kb, qs[hh], nt, preferred_element_type=F32) + bias_ref[hh, dist]
        return s, hit, jnp.where(hit, jnp.max(s, axis=0, keepdims=True), NEG_INF)

    grp = MOBA_GROUP

    def body(step, carry):
        first = (i // grp - step) * grp
        new = []
        all_tiles = [[tile(first + g, hh) for g in range(grp)] for hh in range(2)]
        for hh in range(2):
            m_run, l_run, acc = carry[hh]
            tiles = all_tiles[hh]
            m_new = functools.reduce(jnp.maximum, [m_run] + [tl[2] for tl in tiles])
            l_new = jnp.exp(m_run - m_new) * l_run
            acc_new = jnp.exp(m_run - m_new) * acc
            for g, (s, hit, _) in enumerate(tiles):
                p = jnp.exp(s - jnp.where(hit, m_new, m_new - NEG_INF))
                l_new = l_new + jnp.sum(p, axis=0, keepdims=True)
                acc_new = acc_new + jnp.dot(vt_ref[0, first + g], p.astype(BF16), preferred_element_type=F32)
            new.append((m_new, l_new, acc_new))
        return tuple(new)

    init = tuple((jnp.full((1, blk), NEG_INF, F32), jnp.zeros((1, blk), F32), jnp.zeros((LANES, blk), F32))
                 for _ in range(2))
    fin = lax.fori_loop(0, i // grp + 1, body, init)
    o0 = fin[0][2] / fin[0][1]
    o1 = fin[1][2] / fin[1][1]
    out_t = jnp.concatenate([o0[:HEAD_DIM], o1[HEAD_DIM:]], axis=0)
    o_ref[...] = out_t.T.astype(o_ref.dtype)


def _moba(q, k, vt, bias, batch, seq):
    t = q.shape[0]
    nb = seq // MOBA_BLOCK
    n_slots = min(MOBA_TOPK, nb - 1)
    pairs = ATTN_HEADS // 2
    qo = pl.BlockSpec((MOBA_BLOCK, LANES), lambda hp, b, i: (b * nb + i, hp))
    return pl.pallas_call(
        functools.partial(_moba_kernel, nb=nb, n_slots=n_slots),
        grid=(pairs, batch, nb),
        in_specs=[qo,
                  pl.BlockSpec((seq, LANES), lambda hp, b, i: (b, hp)),
                  pl.BlockSpec((1, nb, LANES, MOBA_BLOCK), lambda hp, b, i: (b, 0, hp, 0)),
                  pl.BlockSpec((2, BIAS_TILES, MOBA_BLOCK, MOBA_BLOCK), lambda hp, b, i: (hp, 0, 0, 0))],
        out_specs=qo,
        out_shape=jax.ShapeDtypeStruct((t, ATTN_WIDTH), BF16),
        scratch_shapes=[pltpu.VMEM((nb, LANES), F32)],
        compiler_params=_cparams("arbitrary", "arbitrary", "arbitrary"),
    )(q, k, vt, bias)


MERGE_TM = 256


def _merge_kernel(x_ref, ya_ref, yg_ref, ga_ref, gg_ref, gt1_ref, sc2_ref, sh2_ref,
                  pg1_ref, g2_ref, wa_ref, wg_ref, wo_ref, wq_ref, sk0_ref, sk1_ref,
                  x1_ref, h2_ref, st_ref):
    nt = (((1,), (1,)), ((), ()))
    la = jnp.dot(ya_ref[...], wa_ref[...], preferred_element_type=F32)
    lg = jnp.dot(yg_ref[...], wg_ref[...], preferred_element_type=F32)
    merged = jax.nn.sigmoid(ga_ref[...]) * la + jax.nn.sigmoid(gg_ref[...]) * lg
    y = jnp.dot(merged.astype(BF16), wo_ref[...], preferred_element_type=F32)
    x1 = x_ref[...] + gt1_ref[0] * _rms(y, pg1_ref[...])
    x1_ref[...] = x1
    h2 = _rms(x1, g2_ref[...]) * (1.0 + sc2_ref[0]) + sh2_ref[0]
    h2_ref[...] = h2
    qp = jnp.dot(h2.astype(BF16), wq_ref[...], preferred_element_type=F32).astype(BF16)
    for h in range(PEER_HEADS):
        qh = qp[:, h * PEER_QDIM:(h + 1) * PEER_QDIM]
        st_ref[2 * h] = lax.dot_general(sk0_ref[...], qh, nt, preferred_element_type=F32)
        st_ref[2 * h + 1] = lax.dot_general(sk1_ref[...], qh, nt, preferred_element_type=F32)


def _merge(x2, ya, yg, ga, gg, gt1, sc2, sh2, post_g1, pre_g2, wa, wg, wo, wq, sk0, sk1, seq):
    t = x2.shape[0]
    tm = MERGE_TM
    per_b = seq // tm
    row = lambda w: pl.BlockSpec((tm, w), lambda i: (i, 0))
    mod = pl.BlockSpec((1, 1, D_MODEL), lambda i: (i // per_b, 0, 0))
    vec = pl.BlockSpec((1, D_MODEL), lambda i: (0, 0))
    full = lambda a: pl.BlockSpec(a.shape, lambda i: (0,) * a.ndim)
    return pl.pallas_call(
        _merge_kernel,
        grid=(t // tm,),
        in_specs=[row(D_MODEL), row(ATTN_WIDTH), row(GMLP_WIDTH), row(D_MODEL), row(D_MODEL),
                  mod, mod, mod, vec, vec, full(wa), full(wg), full(wo), full(wq), full(sk0), full(sk1)],
        out_specs=[row(D_MODEL), row(D_MODEL),
                   pl.BlockSpec((2 * PEER_HEADS, PEER_NKEYS, tm), lambda i: (0, 0, i))],
        out_shape=[jax.ShapeDtypeStruct((t, D_MODEL), F32), jax.ShapeDtypeStruct((t, D_MODEL), F32),
                   jax.ShapeDtypeStruct((2 * PEER_HEADS, PEER_NKEYS, t), F32)],
        compiler_params=_cparams("arbitrary"),
    )(x2, ya, yg, ga, gg, gt1, sc2, sh2, post_g1, pre_g2, wa, wg, wo, wq, sk0, sk1)


ROUTE_SUB = 8
_CANDS = [(a, b) for a in range(PEER_TOPK) for b in range(PEER_TOPK) if (a + 1) * (b + 1) <= PEER_TOPK]


def _route_kernel(s_ref, e_ref, g_ref, buf, buf2):
    kio = lax.broadcasted_iota(I32, (PEER_NKEYS, ROUTE_SUB, LANES), 0).astype(F32)

    def fold(pair_op, reduce_op, a):
        q = PEER_NKEYS // 4
        return reduce_op(pair_op(pair_op(a[0:q], a[q:2 * q]), pair_op(a[2 * q:3 * q], a[3 * q:4 * q])), axis=0)

    def step(buf):
        s = buf[...]
        m = fold(jnp.maximum, jnp.max, s)
        idx = fold(jnp.minimum, jnp.min, jnp.where(s == m[None], kio, float(PEER_NKEYS)))
        buf[...] = jnp.where(kio == idx[None], -jnp.inf, s)
        return m, idx

    buf[...] = s_ref[0]
    buf2[...] = s_ref[1]
    v1, i1, v2, i2 = [], [], [], []
    for _ in range(PEER_TOPK):
        m, idx = step(buf)
        v1.append(m)
        i1.append(idx)
        m, idx = step(buf2)
        v2.append(m)
        i2.append(idx)
    cand = [v1[a] + v2[b] for a, b in _CANDS]
    eid = [i1[a] * float(PEER_NKEYS) + i2[b] for a, b in _CANDS]
    flat = [float(a * PEER_TOPK + b) for a, b in _CANDS]
    sc, es = [], []
    for _ in range(PEER_TOPK):
        m = functools.reduce(jnp.maximum, cand)
        idx = functools.reduce(jnp.minimum,
                               [jnp.where(c == m, f, float(PEER_TOPK * PEER_TOPK)) for c, f in zip(cand, flat)])
        es.append(functools.reduce(jnp.maximum, [jnp.where(idx == f, e, -1.0) for e, f in zip(eid, flat)]))
        cand = [jnp.where(idx == f, -jnp.inf, c) for c, f in zip(cand, flat)]
        sc.append(m)
    ex = [jnp.exp(s - sc[0]) for s in sc]
    tot = functools.reduce(jnp.add, ex)
    for r in range(PEER_TOPK):
        e_ref[0, r] = es[r].astype(I32)
        g_ref[0, r] = ex[r] / tot


def _route(st4):
    ntile = st4.shape[2]
    grp = ROUTE_SUB
    out = pl.BlockSpec((1, PEER_TOPK, grp, LANES), lambda h, g: (h, 0, g, 0))
    shp = (PEER_HEADS, PEER_TOPK, ntile, LANES)
    return pl.pallas_call(
        _route_kernel,
        grid=(PEER_HEADS, ntile // grp),
        in_specs=[pl.BlockSpec((2, PEER_NKEYS, grp, LANES), lambda h, g: (h, 0, g, 0))],
        out_specs=[out, out],
        out_shape=[jax.ShapeDtypeStruct(shp, I32), jax.ShapeDtypeStruct(shp, F32)],
        scratch_shapes=[pltpu.VMEM((PEER_NKEYS, grp, LANES), F32)] * 2,
        compiler_params=_cparams("arbitrary", "arbitrary"),
    )(st4)


PEER_TT = 256
PEER_NBUF = 8
SUBLANES = 8
PEER_KG = PEER_PICKS // SUBLANES
PEER_RT = D_MODEL // LANES
_ORDER = (3, 2, 1, 0, 7, 6, 5, 4)
PEER_F1 = 6
PEER_F2 = PEER_PICKS // (PEER_PICKS - PEER_KG * PEER_F1)


def _pack_rows(u, v):
    ub = lax.bitcast_convert_type(u.astype(BF16), jnp.uint16).astype(jnp.uint32)
    vb = lax.bitcast_convert_type(v.astype(BF16), jnp.uint16).astype(jnp.uint32)
    return (ub | (vb << 16)).reshape(u.shape[0], PEER_RT, LANES)


def _peer_kernel(e_ref, h2_ref, gt_ref, x1_ref, gt2_ref, pg2_ref, uv_ref, o_ref, *scratch):
    bufs = scratch[:PEER_NBUF]
    sem, y_sc, w_sc = scratch[PEER_NBUF:]
    tt = PEER_TT
    ahead = PEER_NBUF - 1

    def issue(t, slot, picks=range(PEER_PICKS)):
        for k in picks:
            pltpu.make_async_copy(uv_ref.at[e_ref[t, k]], bufs[slot].at[k], sem.at[slot]
                                  ).start(priority=k % 2)

    def wait(slot):
        pltpu.make_async_copy(bufs[slot], bufs[slot], sem.at[slot]).wait()

    lane_t = lax.broadcasted_iota(I32, (PEER_PICKS, tt), 1)
    himask = jnp.uint32(0xFFFF0000)
    row = lax.broadcasted_iota(I32, (SUBLANES, LANES), 0)
    top4 = row < 4
    hi2 = (row & 2) != 0
    odd = (row & 1) != 0
    roll = lambda a, s: pltpu.roll(a, s, 0)

    def sublane_sums(a):
        b = [jnp.where(top4, a[j], a[j + 4]) + roll(jnp.where(top4, a[j + 4], a[j]), 4) for j in range(4)]
        c = [jnp.where(hi2, b[j], roll(b[j + 2], 6)) + jnp.where(hi2, roll(b[j], 2), b[j + 2]) for j in range(2)]
        return jnp.where(odd, c[0], roll(c[1], 7)) + jnp.where(odd, roll(c[0], 1), c[1])

    def compute(t, slot, fetch=None):
        buf = bufs[slot]
        xb = h2_ref[pl.ds(t, 1), :]
        x_tile = jnp.concatenate([xb[:, r * LANES:(r + 1) * LANES] for r in range(PEER_RT)], axis=0)
        parts = []
        for kg in range(PEER_KG):
            prods = [lax.bitcast_convert_type(buf[kg * SUBLANES + _ORDER[j]] << 16, F32) * x_tile
                     for j in range(SUBLANES)]
            parts.append(sublane_sums(prods))
            if fetch is not None:
                fetch(range(kg * PEER_F1, (kg + 1) * PEER_F1))
        a = jnp.sum(jnp.concatenate(parts, axis=0), axis=-1, keepdims=True)
        gcol = jnp.sum(jnp.where(lane_t == t, gt_ref[...], 0.0), axis=-1, keepdims=True)
        w_sc[...] = jnp.broadcast_to(gcol * jax.nn.gelu(a), (PEER_PICKS, LANES))
        accs = [None] * 4
        for k in range(PEER_PICKS):
            term = w_sc[pl.ds(k, 1), :] * lax.bitcast_convert_type(buf[k] & himask, F32)
            accs[k % 4] = term if accs[k % 4] is None else accs[k % 4] + term
            if fetch is not None and k % PEER_F2 == 0:
                fetch((PEER_KG * PEER_F1 + k // PEER_F2,))
        y_tile = (accs[0] + accs[1]) + (accs[2] + accs[3])
        y_sc[pl.ds(t, 1), :] = jnp.concatenate([y_tile[s:s + 1] for s in range(SUBLANES)], axis=-1)

    for t in range(ahead):
        issue(t, t)

    def group(g, c):
        for s in range(PEER_NBUF):
            t = g * PEER_NBUF + s
            wait(s)
            compute(t, s, functools.partial(issue, t + ahead, (s + ahead) % PEER_NBUF))
        return c

    ngroups = tt // PEER_NBUF
    lax.fori_loop(0, ngroups - 1, group, 0)
    for s in range(PEER_NBUF):
        t = (ngroups - 1) * PEER_NBUF + s
        if s == 0:
            issue(t + ahead, ahead)
        wait(s)
        compute(t, s)
    o_ref[...] = x1_ref[...] + gt2_ref[0] * _rms(y_sc[...], pg2_ref[...])


def _peer(e_t, h2, g_t, x1, gt2, post_g2, uv, seq):
    t = h2.shape[0]
    tt = PEER_TT
    per_b = seq // tt
    row = pl.BlockSpec((tt, D_MODEL), lambda i: (i, 0))
    return pl.pallas_call(
        _peer_kernel,
        grid=(t // tt,),
        in_specs=[pl.BlockSpec((tt, PEER_PICKS), lambda i: (i, 0), memory_space=pltpu.SMEM),
                  row,
                  pl.BlockSpec((PEER_PICKS, tt), lambda i: (0, i)),
                  row,
                  pl.BlockSpec((1, 1, D_MODEL), lambda i: (i // per_b, 0, 0)),
                  pl.BlockSpec((1, D_MODEL), lambda i: (0, 0)),
                  pl.BlockSpec(memory_space=pl.ANY)],
        out_specs=row,
        out_shape=jax.ShapeDtypeStruct((t, D_MODEL), F32),
        scratch_shapes=[pltpu.VMEM((PEER_PICKS, PEER_RT, LANES), jnp.uint32) for _ in range(PEER_NBUF)]
        + [pltpu.SemaphoreType.DMA((PEER_NBUF,)), pltpu.VMEM((tt, D_MODEL), F32),
           pltpu.VMEM((PEER_PICKS, LANES), F32)],
        compiler_params=_cparams("arbitrary"),
    )(e_t, h2, g_t, x1, gt2, post_g2, uv)


def kernel(x, c, ada_w, ada_b, pre_g1, post_g1, w_in, gmlp_ln_g, gmlp_ln_b, gmlp_w_s, gmlp_b_s,
           rel_bias, w_branch_attn, w_branch_gmlp, w_out, pre_g2, post_g2, peer_w_query,
           peer_sub_keys, peer_u, peer_v):
    batch, seq, d = x.shape
    depth = ada_w.shape[0]
    t = batch * seq
    assert d == D_MODEL and batch <= 8 and seq % (MOBA_GROUP * MOBA_BLOCK) == 0 and t % (ROUTE_SUB * LANES) == 0

    c8 = jnp.zeros((8, d), F32).at[:batch].set(c)
    bias = _bias_tiles(rel_bias)
    x2 = x.reshape(t, d)
    for l in range(depth):
        mod = _ada(c8, ada_w[l], ada_b[l][None, :])[:batch]
        sh1, sc1, gt1, sh2, sc2, gt2 = [m[:, None, :] for m in jnp.split(mod, 6, axis=-1)]

        w_in_bf = w_in[l].astype(BF16)
        wvt = w_in_bf[:, 2 * ATTN_WIDTH:3 * ATTN_WIDTH].T
        q, k, v, zu, zv, ga, gg = _inproj(x2, sc1, sh1, pre_g1[l][None, :], w_in_bf, wvt, batch, seq)
        yg = _gmlp(zu, zv, gmlp_ln_g[l][None, :], gmlp_ln_b[l][None, :], gmlp_w_s[l], gmlp_b_s[l].T)
        ya = _moba(q, k, v, bias, batch, seq)

        zpad = jnp.zeros((PEER_NKEYS, PEER_HALF), F32)
        sk0 = jnp.concatenate([peer_sub_keys[l, 0], zpad], axis=1).astype(BF16)
        sk1 = jnp.concatenate([zpad, peer_sub_keys[l, 1]], axis=1).astype(BF16)
        x1, h2, st = _merge(x2, ya, yg, ga, gg, gt1, sc2, sh2, post_g1[l][None, :], pre_g2[l][None, :],
                            w_branch_attn[l].astype(BF16), w_branch_gmlp[l].astype(BF16),
                            w_out[l].astype(BF16), peer_w_query[l].astype(BF16), sk0, sk1, seq)

        e4, g4 = _route(st.reshape(2 * PEER_HEADS, PEER_NKEYS, t // LANES, LANES))
        e_t = e4.reshape(PEER_PICKS, t).T
        g_t = g4.reshape(PEER_PICKS, t)
        uv = _pack_rows(peer_u[l], peer_v[l])
        x2 = _peer(e_t, h2, g_t, x1, gt2, post_g2[l][None, :], uv, seq)
    return x2.reshape(batch, seq, d)
```
